```python
import numpy as np
import jax, jax.numpy as jnp
from jax import lax

D_MODEL = 2048
BATCH = 2
SEQ = 8192
DEPTH = 1

NSA_HEADS = 8
NSA_KV_GROUPS = 2
NSA_HG = NSA_HEADS // NSA_KV_GROUPS
NSA_DK = 128
NSA_WIDTH = NSA_HEADS * NSA_DK
KV_W = NSA_KV_GROUPS * NSA_DK
CMP_BLOCK = 32
CMP_STRIDE = 16
SEL_BLOCK = 64
SEL_TOPN = 16
WINDOW = 512
Q_BLOCK = 128
FORCE = 1e4
NEG = -1e30
RET_HEADS = 4
RET_DK = 256
RET_DV = 256
RET_WIDTH = RET_HEADS * RET_DV
RET_CHUNK = 128
ROPE_BASE = 10000.0
GN_EPS = 1e-6
MIX_WIDTH = NSA_WIDTH + RET_WIDTH
T5_BUCKETS = 32
T5_MAX_DIST = 128
D_FF = 4 * D_MODEL
NORM_EPS = 1e-6
IN_SIZES = [NSA_WIDTH, 6 * KV_W, 3 * NSA_HEADS, RET_HEADS * RET_DK, RET_HEADS * RET_DK, RET_WIDTH, RET_WIDTH]
D_IN = sum(IN_SIZES)
IN_OFFSETS = np.cumsum(IN_SIZES)[:-1].tolist()

kernel_name = "hymba_nsa_retention_sandwich_block"


def rms_norm(x, g):
    xf = x.astype(jnp.float32)
    y = xf * lax.rsqrt(jnp.mean(xf * xf, axis=-1, keepdims=True) + NORM_EPS)
    return (y * g.astype(jnp.float32)).astype(x.dtype)


def t5_bucket(rel):
    n = jnp.maximum(rel, 0)
    max_exact = T5_BUCKETS // 2
    nf = jnp.maximum(n, 1).astype(jnp.float32)
    large = max_exact + (jnp.log(nf / max_exact) / np.log(T5_MAX_DIST / max_exact)
                         * (T5_BUCKETS - max_exact)).astype(jnp.int32)
    large = jnp.minimum(large, T5_BUCKETS - 1)
    return jnp.where(n < max_exact, n, large)


def rope(x, pos):
    d = x.shape[-1]
    inv_freq = ROPE_BASE ** (-jnp.arange(0, d, 2, dtype=jnp.float32) / d)
    ang = pos.astype(jnp.float32)[:, None] * inv_freq[None, :]
    cos = jnp.cos(ang)[None, :, None, :]
    sin = jnp.sin(ang)[None, :, None, :]
    xf = x.astype(jnp.float32)
    x1, x2 = xf[..., : d // 2], xf[..., d // 2:]
    return jnp.concatenate([x1 * cos - x2 * sin, x2 * cos + x1 * sin], axis=-1).astype(x.dtype)


def compress_kv(kv, pe, w1, b1, w2):
    B, S, G, dk = kv.shape
    n_cmp = (S - CMP_BLOCK) // CMP_STRIDE + 1
    idx = jnp.arange(n_cmp)[:, None] * CMP_STRIDE + jnp.arange(CMP_BLOCK)[None, :]
    blocks = kv[:, idx] + pe[None, None, :, None, :]
    flat = blocks.transpose(0, 1, 3, 2, 4).reshape(B, n_cmp, G, CMP_BLOCK * dk)
    hid = jax.nn.gelu(flat @ w1 + b1)
    return hid @ w2


def nsa_attention(q, k_cmp, v_cmp, k_slc, v_slc, k_win, v_win, gates, t5_table):
    B, S, H, dk = q.shape
    G, Hg = NSA_KV_GROUPS, NSA_HG
    n_cmp = k_cmp.shape[1]
    n_sel = S // SEL_BLOCK
    top_n = min(SEL_TOPN, n_sel)
    scale = dk ** -0.5
    cmp_end = jnp.arange(n_cmp) * CMP_STRIDE + CMP_BLOCK - 1
    cs = np.arange(n_cmp) * CMP_STRIDE
    ss = np.arange(n_sel) * SEL_BLOCK
    overlap = (cs[:, None] <= ss[None, :] + SEL_BLOCK - 1) & (cs[:, None] + CMP_BLOCK - 1 >= ss[None, :])
    m_sel = jnp.asarray(overlap, jnp.float32)
    tbl = t5_table.astype(jnp.float32).reshape(T5_BUCKETS, G, Hg)
    ks_blocks = k_slc.reshape(B, n_sel, SEL_BLOCK, G, dk).transpose(0, 3, 1, 2, 4)
    vs_blocks = v_slc.reshape(B, n_sel, SEL_BLOCK, G, dk).transpose(0, 3, 1, 2, 4)
    kw_pad = jnp.pad(k_win, ((0, 0), (WINDOW, 0), (0, 0), (0, 0)))
    vw_pad = jnp.pad(v_win, ((0, 0), (WINDOW, 0), (0, 0), (0, 0)))
    qg = q.reshape(B, S, G, Hg, dk)
    gg = gates.reshape(B, S, G, Hg, 3)
    b_idx = jnp.arange(B)[:, None, None, None]
    g_idx = jnp.arange(G)[None, :, None, None]
    g_idx5 = jnp.arange(G)[None, :, None, None, None]
    jblk = jnp.arange(n_sel)

    def block(c):
        s0 = c * Q_BLOCK
        qc = lax.dynamic_slice_in_dim(qg, s0, Q_BLOCK, axis=1)
        tq = s0 + jnp.arange(Q_BLOCK)
        rel = tq[:, None] - cmp_end[None, :]
        s = jnp.einsum('bqghd,bngd->bghqn', qc, k_cmp).astype(jnp.float32) * scale
        s = s + tbl[t5_bucket(rel)].transpose(2, 3, 0, 1)[None]
        valid = rel >= 0
        p = jax.nn.softmax(jnp.where(valid, s, NEG), axis=-1)
        p = jnp.where(valid, p, 0.0)
        o_cmp = jnp.einsum('bghqn,bngd->bqghd', p.astype(v_cmp.dtype), v_cmp)
        imp = jnp.einsum('bghqn,nj->bgqj', p, m_sel)
        cur = tq // SEL_BLOCK
        blk_valid = jblk[None, :] * SEL_BLOCK <= tq[:, None]
        forced = (jblk[None, :] == 0) | (jblk[None, :] == cur[:, None]) | (jblk[None, :] == cur[:, None] - 1)
        imp = jnp.where(blk_valid, jnp.where(forced, FORCE, imp), -1.0)
        top_val, top_idx = lax.top_k(imp, top_n)
        ksel = ks_blocks[b_idx, g_idx, top_idx]
        vsel = vs_blocks[b_idx, g_idx, top_idx]
        kpos = top_idx[..., None] * SEL_BLOCK + jnp.arange(SEL_BLOCK)
        rel = tq[None, None, :, None, None] - kpos
        s = jnp.einsum('bqghd,bgqnld->bghqnl', qc, ksel).astype(jnp.float32) * scale
        s = s + tbl[t5_bucket(rel), g_idx5].transpose(0, 1, 5, 2, 3, 4)
        valid = ((rel >= 0) & (top_val >= 0)[..., None])[:, :, None]
        s = jnp.where(valid, s, NEG).reshape(B, G, Hg, Q_BLOCK, top_n * SEL_BLOCK)
        p = jax.nn.softmax(s, axis=-1).reshape(B, G, Hg, Q_BLOCK, top_n, SEL_BLOCK)
        o_slc = jnp.einsum('bghqnl,bgqnld->bqghd', p.astype(vsel.dtype), vsel)
        kw = lax.dynamic_slice_in_dim(kw_pad, s0, WINDOW + Q_BLOCK, axis=1)
        vw = lax.dynamic_slice_in_dim(vw_pad, s0, WINDOW + Q_BLOCK, axis=1)
        kpos = s0 - WINDOW + jnp.arange(WINDOW + Q_BLOCK)
        rel = tq[:, None] - kpos[None, :]
        valid = (rel >= 0) & (rel < WINDOW) & (kpos[None, :] >= 0)
        s = jnp.einsum('bqghd,bkgd->bghqk', qc, kw).astype(jnp.float32) * scale
        s = s + tbl[t5_bucket(rel)].transpose(2, 3, 0, 1)[None]
        p = jax.nn.softmax(jnp.where(valid, s, NEG), axis=-1)
        o_win = jnp.einsum('bghqk,bkgd->bqghd', p.astype(vw.dtype), vw)
        gc = lax.dynamic_slice_in_dim(gg, s0, Q_BLOCK, axis=1)
        o = gc[..., 0:1] * o_cmp + gc[..., 1:2] * o_slc + gc[..., 2:3] * o_win
        return o.reshape(B, Q_BLOCK, H * dk)

    out = lax.map(block, jnp.arange(S // Q_BLOCK))
    return out.transpose(1, 0, 2, 3).reshape(B, S, H * dk)


def retention(q, k, v, log_gamma):
    B, S, H, dk = q.shape
    dv = v.shape[-1]
    C = RET_CHUNK
    N = S // C
    k = k * (dk ** -0.5)
    qc = q.reshape(B, N, C, H, dk).transpose(1, 0, 3, 2, 4)
    kc = k.reshape(B, N, C, H, dk).transpose(1, 0, 3, 2, 4)
    vc = v.reshape(B, N, C, H, dv).transpose(1, 0, 3, 2, 4)
    idx = jnp.arange(C, dtype=jnp.float32)
    diff = idx[:, None] - idx[None, :]
    decay = jnp.where(diff[None] >= 0, jnp.exp(jnp.maximum(diff, 0.0)[None] * log_gamma[:, None, None]), 0.0)
    xi = jnp.exp((idx + 1.0)[None, :] * log_gamma[:, None])
    zeta = jnp.exp((C - 1.0 - idx)[None, :] * log_gamma[:, None])
    g_c = jnp.exp(C * log_gamma)
    inner = jnp.einsum('nbhcd,nbhmd->nbhcm', qc, kc).astype(jnp.float32) * decay
    inner = jnp.einsum('nbhcm,nbhme->nbhce', inner, vc.astype(jnp.float32))

    def step(R, xs):
        q_n, k_n, v_n = xs
        cross = jnp.einsum('bhcd,bhde->bhce', q_n.astype(jnp.float32), R) * xi[None, :, :, None]
        R = R * g_c[None, :, None, None] + jnp.einsum(
            'bhcd,bhce->bhde', k_n.astype(jnp.float32) * zeta[None, :, :, None], v_n.astype(jnp.float32))
        return R, cross

    R0 = jnp.zeros((B, H, dk, dv), jnp.float32)
    _, cross = lax.scan(step, R0, (qc, kc, vc))
    o = (inner + cross).transpose(1, 0, 3, 2, 4).reshape(B, S, H, dv)
    mu = jnp.mean(o, axis=-1, keepdims=True)
    var = jnp.mean(jnp.square(o - mu), axis=-1, keepdims=True)
    return (o - mu) * lax.rsqrt(var + GN_EPS)


def setup_inputs(seed: int = 0) -> dict:
    key = jax.random.key(seed)
    ks = jax.random.split(key, 20)
    f32 = jnp.float32

    def nrm(k, shape, scale):
        return jax.random.normal(k, shape, f32) * scale

    def gain(k):
        return 1.0 + nrm(k, (DEPTH, D_MODEL), 0.05)

    LDK = CMP_BLOCK * NSA_DK
    return {
        "x": nrm(ks[0], (BATCH, SEQ, D_MODEL), 1.0),
        "norm_mix_pre": gain(ks[1]),
        "w_in": nrm(ks[2], (DEPTH, D_MODEL, D_IN), D_MODEL ** -0.5),
        "cmp_pe_k": nrm(ks[3], (DEPTH, CMP_BLOCK, NSA_DK), 0.1),
        "cmp_w1_k": nrm(ks[4], (DEPTH, LDK, NSA_DK), LDK ** -0.5),
        "cmp_b1_k": nrm(ks[5], (DEPTH, NSA_DK), 0.02),
        "cmp_w2_k": nrm(ks[6], (DEPTH, NSA_DK, NSA_DK), NSA_DK ** -0.5),
        "cmp_pe_v": nrm(ks[7], (DEPTH, CMP_BLOCK, NSA_DK), 0.1),
        "cmp_w1_v": nrm(ks[8], (DEPTH, LDK, NSA_DK), LDK ** -0.5),
        "cmp_b1_v": nrm(ks[9], (DEPTH, NSA_DK), 0.02),
        "cmp_w2_v": nrm(ks[10], (DEPTH, NSA_DK, NSA_DK), NSA_DK ** -0.5),
        "t5_bias": nrm(ks[11], (T5_BUCKETS, NSA_HEADS), 0.2),
        "w_out": nrm(ks[12], (DEPTH, MIX_WIDTH, D_MODEL), MIX_WIDTH ** -0.5),
        "norm_mix_post": gain(ks[13]),
        "norm_mlp_pre": gain(ks[14]),
        "w_up": nrm(ks[15], (DEPTH, D_MODEL, D_FF), D_MODEL ** -0.5),
        "w_down": nrm(ks[16], (DEPTH, D_FF, D_MODEL), D_FF ** -0.5),
        "norm_mlp_post": gain(ks[17]),
    }


def reference(x, norm_mix_pre, w_in, cmp_pe_k, cmp_w1_k, cmp_b1_k, cmp_w2_k, cmp_pe_v, cmp_w1_v,
              cmp_b1_v, cmp_w2_v, t5_bias, w_out, norm_mix_post, norm_mlp_pre, w_up, w_down,
              norm_mlp_post):
    B, S, _ = x.shape
    pos = jnp.arange(S)
    log_gamma = jnp.log(1.0 - jnp.exp2(-5.0 - jnp.arange(RET_HEADS, dtype=jnp.float32)))
    for l in range(DEPTH):
        h = rms_norm(x, norm_mix_pre[l])
        proj = h @ w_in[l]
        q_a, kv_a, gate_a, q_r, k_r, v_r, g_r = jnp.split(proj, IN_OFFSETS, axis=-1)
        q_a = q_a.reshape(B, S, NSA_HEADS, NSA_DK)
        kv6 = kv_a.reshape(B, S, 6, NSA_KV_GROUPS, NSA_DK)
        k_cmp = compress_kv(kv6[:, :, 0], cmp_pe_k[l], cmp_w1_k[l], cmp_b1_k[l], cmp_w2_k[l])
        v_cmp = compress_kv(kv6[:, :, 1], cmp_pe_v[l], cmp_w1_v[l], cmp_b1_v[l], cmp_w2_v[l])
        gates = jax.nn.sigmoid(gate_a.reshape(B, S, NSA_HEADS, 3))
        o_a = nsa_attention(q_a, k_cmp, v_cmp, kv6[:, :, 2], kv6[:, :, 3], kv6[:, :, 4], kv6[:, :, 5],
                            gates, t5_bias)
        q_r = rope(q_r.reshape(B, S, RET_HEADS, RET_DK), pos)
        k_r = rope(k_r.reshape(B, S, RET_HEADS, RET_DK), pos)
        o_r = retention(q_r, k_r, v_r.reshape(B, S, RET_HEADS, RET_DV), log_gamma)
        o_r = (jax.nn.silu(g_r.astype(jnp.float32)) * o_r.reshape(B, S, RET_WIDTH)).astype(x.dtype)
        mix = jnp.concatenate([o_a.astype(x.dtype), o_r], axis=-1) @ w_out[l]
        x = x + rms_norm(mix, norm_mix_post[l])
        h = rms_norm(x, norm_mlp_pre[l])
        u = jnp.square(jax.nn.relu(h @ w_up[l]))
        x = x + rms_norm(u @ w_down[l], norm_mlp_post[l])
    return x
```

```python
import functools

import numpy as np
import jax
import jax.numpy as jnp
from jax import lax
from jax.experimental import pallas as pl
from jax.experimental.pallas import tpu as pltpu

F32 = jnp.float32
BF16 = jnp.bfloat16

D_MODEL = 2048
NSA_HEADS = 8
NSA_G = 2
NSA_HG = 4
DK = 128
CMP_BLOCK = 32
CMP_STRIDE = 16
SEL_BLOCK = 64
SEL_TOPN = 16
WINDOW = 512
QB = 128
RET_HEADS = 4
RET_D = 256
RET_CHUNK = 128
ROPE_BASE = 10000.0
GN_EPS = 1e-6
NORM_EPS = 1e-6
T5_BUCKETS = 32
T5_MAX_DIST = 128
D_FF = 4 * D_MODEL
FORCE = 1e4
NEG = -1e30
M_FLOOR = -1e20

COL_QA = 0
COL_QR = 1024
COL_KR = 2048
COL_VR = 3072
COL_GR = 4096
COL_KV = 5120
COL_GATE = 6656
N_PROJ = 7168
CMP_PAD = 120

VMEM_LIMIT = 56 * 1024 * 1024


def _dot(a, b):
    return jnp.dot(a, b, preferred_element_type=F32)


def _dot_nt(a, b):
    return lax.dot_general(a, b, (((1,), (1,)), ((), ())), preferred_element_type=F32)


def _inproj_kernel(x_ref, g_ref, w_ref, cs_ref, o_ref, h_scr):
    @pl.when(pl.program_id(1) == 0)
    def _():
        xf = x_ref[...]
        ms = jnp.mean(xf * xf, axis=-1, keepdims=True)
        h_scr[...] = (xf * lax.rsqrt(ms + NORM_EPS) * g_ref[...]).astype(BF16)

    acc = _dot(h_scr[...], w_ref[...])
    o_ref[...] = (acc * cs_ref[...]).astype(o_ref.dtype)


def _in_proj(x2, gain, w, colscale, tm=1024, tn=512):
    T, D = x2.shape
    N = w.shape[1]
    tm = min(tm, T)
    return pl.pallas_call(
        _inproj_kernel,
        grid=(T // tm, N // tn),
        in_specs=[
            pl.BlockSpec((tm, D), lambda i, j: (i, 0)),
            pl.BlockSpec((1, D), lambda i, j: (0, 0)),
            pl.BlockSpec((D, tn), lambda i, j: (0, j)),
            pl.BlockSpec((1, tn), lambda i, j: (0, j)),
        ],
        out_specs=pl.BlockSpec((tm, tn), lambda i, j: (i, j)),
        out_shape=jax.ShapeDtypeStruct((T, N), BF16),
        scratch_shapes=[pltpu.VMEM((tm, D), BF16)],
        compiler_params=pltpu.CompilerParams(
            dimension_semantics=("parallel", "arbitrary"), vmem_limit_bytes=VMEM_LIMIT),
        name="in_proj",
    )(x2, gain, w, colscale)


def _compress_kernel(x_ref, pe_ref, w1_ref, b1_ref, w2_ref, o_ref):
    nc = x_ref.shape[2]
    half = x_ref.shape[3]
    xf = x_ref[0, 0].astype(F32)
    xa = (xf + pe_ref[0, 0:1, :]).astype(BF16)
    xb = (xf + pe_ref[0, 1:2, :]).astype(BF16)
    a = _dot(xa, w1_ref[0, :half, :])
    b = _dot(xb, w1_ref[0, half:, :])
    b_next = pltpu.roll(b, nc - 1, 0)
    hid = jax.nn.gelu(a + b_next + b1_ref[0])
    out = _dot(hid.astype(BF16), w2_ref[0])
    o_ref[0, 0, :CMP_PAD, :] = jnp.zeros((CMP_PAD, DK), F32)
    o_ref[0, 0, CMP_PAD:CMP_PAD + nc, :] = out
    o_ref[0, 0, CMP_PAD + nc:, :] = jnp.zeros((o_ref.shape[2] - CMP_PAD - nc, DK), F32)


def _compress(xc, pe, w1, b1, w2):
    B, _, NC, half = xc.shape
    rows = CMP_PAD + NC + 8
    return pl.pallas_call(
        _compress_kernel,
        grid=(B, 4),
        in_specs=[
            pl.BlockSpec((1, 1, NC, half), lambda b, s: (b, s, 0, 0)),
            pl.BlockSpec((1, 8, half), lambda b, s: (s // 2, 0, 0)),
            pl.BlockSpec((1, 2 * half, DK), lambda b, s: (s // 2, 0, 0)),
            pl.BlockSpec((1, 1, DK), lambda b, s: (s // 2, 0, 0)),
            pl.BlockSpec((1, DK, DK), lambda b, s: (s // 2, 0, 0)),
        ],
        out_specs=pl.BlockSpec((1, 1, rows, DK), lambda b, s: (b, s, 0, 0)),
        out_shape=jax.ShapeDtypeStruct((B, 4, rows, DK), F32),
        compiler_params=pltpu.CompilerParams(
            dimension_semantics=("parallel", "parallel"), vmem_limit_bytes=VMEM_LIMIT),
        name="compress_kv",
    )(xc, pe, w1, b1, w2)


def _nsa_kernel(q_ref, ksl_ref, vsl_ref, kw_ref, vw_ref, gate_ref, kc_ref, vc_ref,
                msel_ref, e_ref, bnear_ref, wtab_ref, tn_ref, cb_ref, o_ref,
                acc_ref, m_ref, l_ref, out_ref):
    qb = pl.program_id(2)
    q = q_ref[...]
    qs = jnp.concatenate([q[:, h * DK:(h + 1) * DK] for h in range(NSA_HG)], axis=0)
    rows = NSA_HG * QB
    lane = lax.broadcasted_iota(jnp.int32, (1, 128), 1)
    ones_kv = jnp.ones((256, DK), BF16)

    gt = jax.nn.sigmoid(gate_ref[...].astype(F32))

    def gate_rows(c):
        return jnp.concatenate(
            [jnp.broadcast_to(gt[:, 3 * h + c:3 * h + c + 1], (QB, DK)) for h in range(NSA_HG)], axis=0)

    m_ref[...] = jnp.full((rows, 1), M_FLOOR, F32)
    l_ref[...] = jnp.zeros((rows, 1), F32)
    acc_ref[...] = jnp.zeros((rows, 256), F32)

    def cmp_chunk(start, ms_start, bias):
        kc = kc_ref[0, 0, pl.ds(start, 128), :].astype(BF16)
        vc = vc_ref[0, 0, pl.ds(start, 128), :].astype(BF16)
        ms = msel_ref[pl.ds(ms_start, 128), :].astype(BF16)
        s = _dot_nt(qs, kc) + bias
        m_old = m_ref[...]
        m_new = jnp.maximum(m_old, jnp.max(s, axis=-1, keepdims=True))
        alpha = jnp.exp(m_old - m_new)
        p = jnp.exp(s - m_new)
        l_ref[...] = alpha * l_ref[...] + jnp.sum(p, axis=-1, keepdims=True)
        rhs = jnp.concatenate([vc, ms], axis=1)
        acc_ref[...] = alpha * acc_ref[...] + _dot(p.astype(BF16), rhs)
        m_ref[...] = m_new

    near_start = pl.multiple_of(8 * qb, 8)
    near_bias = tn_ref[0] + jnp.where(lane + (8 * qb - CMP_PAD) < 0, NEG, 0.0)
    cmp_chunk(near_start, near_start, near_bias)

    def far_body(c, carry):
        st = pl.multiple_of(CMP_PAD + 128 * c, 8)
        bias = cb_ref[0] + jnp.where(128 * c + lane < 8 * qb - CMP_PAD, 0.0, NEG)
        cmp_chunk(st, st, bias)
        return carry

    lax.fori_loop(0, (8 * qb + 7) // 128, far_body, 0)

    l = l_ref[...]
    inv = jnp.where(l > 0.0, 1.0 / l, 0.0)
    accv = acc_ref[...]
    out_ref[...] = gate_rows(0) * (accv[:, :DK] * inv)
    u = accv[:, DK:] * inv
    imp = u[0:QB] + u[QB:2 * QB] + u[2 * QB:3 * QB] + u[3 * QB:4 * QB]

    qi = lax.broadcasted_iota(jnp.int32, (QB, 128), 0)
    ji = lax.broadcasted_iota(jnp.int32, (QB, 128), 1)
    cur = 2 * qb + (qi >= SEL_BLOCK).astype(jnp.int32)
    blk_valid = ji <= cur
    forced = (ji == 0) | (ji == cur) | (ji == cur - 1)
    impm = jnp.where(blk_valid, jnp.where(forced, FORCE, imp), -1.0)
    vt = impm.T
    vt0 = vt
    jio = lax.broadcasted_iota(jnp.int32, (128, QB), 0).astype(F32)
    sel_t = jnp.zeros((128, QB), F32)
    for _ in range(SEL_TOPN):
        mx = jnp.max(vt, axis=0, keepdims=True)
        idx = jnp.min(jnp.where(vt == mx, jio, 128.0), axis=0, keepdims=True)
        hit = jio == idx
        sel_t = jnp.where(hit, 1.0, sel_t)
        vt = jnp.where(hit, -jnp.inf, vt)
    sel_t = jnp.where(vt0 >= 0.0, sel_t, 0.0)
    sel = sel_t.T
    sb = (sel - 1.0) * (-NEG)
    sb_far = jnp.where(lane < 2 * qb - 2, sb, NEG)
    cb = cb_ref[0]
    aug_far = (jnp.concatenate([sb_far] * NSA_HG, axis=0) + cb).astype(BF16)
    aug_near = jnp.concatenate([sb] * NSA_HG, axis=0).astype(BF16)
    q_far = jnp.concatenate([qs, aug_far], axis=1)
    q_near = jnp.concatenate([qs, aug_near], axis=1)

    m_ref[...] = jnp.full((rows, 1), M_FLOOR, F32)
    acc_ref[...] = jnp.zeros((rows, 256), F32)

    def sel_update(s, v256):
        m_old = m_ref[...]
        m_new = jnp.maximum(m_old, jnp.max(s, axis=-1, keepdims=True))
        alpha = jnp.exp(m_old - m_new)
        p = jnp.exp(s - m_new)
        va = jnp.concatenate([v256, ones_kv], axis=1)
        acc_ref[...] = alpha * acc_ref[...] + _dot(p.astype(BF16), va)
        m_ref[...] = m_new

    def sel_far_body(t, carry):
        st = pl.multiple_of(256 * t, 256)
        ka = jnp.concatenate([ksl_ref[pl.ds(st, 256), :], e_ref[pl.ds(st, 256), :]], axis=1)
        sel_update(_dot_nt(q_far, ka), vsl_ref[pl.ds(st, 256), :])
        return carry

    lax.fori_loop(0, qb // 2, sel_far_body, 0)

    st_prev = pl.multiple_of(jnp.maximum(qb - 1, 0) * QB, QB)
    st_diag = pl.multiple_of(qb * QB, QB)
    first_mask = jnp.where(qb == 0, NEG, 0.0)
    near_mask = jnp.concatenate([jnp.full((1, 128), first_mask, F32), jnp.zeros((1, 128), F32)], axis=1)
    bnear = bnear_ref[0] + near_mask
    ka = jnp.concatenate([
        jnp.concatenate([ksl_ref[pl.ds(st_prev, QB), :], e_ref[pl.ds(st_prev, QB), :]], axis=1),
        jnp.concatenate([ksl_ref[pl.ds(st_diag, QB), :], e_ref[pl.ds(st_diag, QB), :]], axis=1)], axis=0)
    v_near = jnp.concatenate([vsl_ref[pl.ds(st_prev, QB), :], vsl_ref[pl.ds(st_diag, QB), :]], axis=0)
    sel_update(_dot_nt(q_near, ka) + bnear, v_near)
    accv = acc_ref[...]
    out_ref[...] += gate_rows(1) * (accv[:, :DK] / accv[:, DK:])

    starts = [pl.multiple_of(jnp.maximum(qb + d, 0) * QB, QB) for d in range(-4, 1)]
    kwin = jnp.concatenate([kw_ref[pl.ds(st, QB), :] for st in starts], axis=0)
    vwin = jnp.concatenate([vw_ref[pl.ds(st, QB), :] for st in starts], axis=0)
    vwa = jnp.concatenate([vwin, jnp.ones((640, DK), BF16)], axis=1)
    dyn = jnp.concatenate(
        [jnp.full((1, 128), jnp.where(qb + d < 0, NEG, 0.0), F32) for d in range(-4, 1)], axis=1)
    s = _dot_nt(qs, kwin) + wtab_ref[0] + dyn
    mw = jnp.max(s, axis=-1, keepdims=True)
    p = jnp.exp(s - mw)
    r = _dot(p.astype(BF16), vwa)
    res = out_ref[...] + gate_rows(2) * (r[:, :DK] / r[:, DK:])
    for h in range(NSA_HG):
        o_ref[:, h * DK:(h + 1) * DK] = res[h * QB:(h + 1) * QB].astype(o_ref.dtype)


def _nsa(proj, kvc, msel, e_mat, bnear, wtab, tnear, cbias, B, S):
    T = B * S
    NQ = S // QB
    kv0 = COL_KV // DK + 4
    rows_c = kvc.shape[2]
    seq = lambda off: pl.BlockSpec((S, DK), lambda b, g, i: (b, off + g))
    tab = lambda shape: pl.BlockSpec((1,) + shape, lambda b, g, i: (g, 0, 0))
    return pl.pallas_call(
        _nsa_kernel,
        grid=(B, NSA_G, NQ),
        in_specs=[
            pl.BlockSpec((QB, NSA_HG * DK), lambda b, g, i: (b * NQ + i, g)),
            seq(kv0), seq(kv0 + 2), seq(kv0 + 4), seq(kv0 + 6),
            pl.BlockSpec((QB, 128), lambda b, g, i: (b * NQ + i, COL_GATE // 128 + g)),
            pl.BlockSpec((1, 1, rows_c, DK), lambda b, g, i: (b, g, 0, 0)),
            pl.BlockSpec((1, 1, rows_c, DK), lambda b, g, i: (b, 2 + g, 0, 0)),
            pl.BlockSpec(msel.shape, lambda b, g, i: (0, 0)),
            pl.BlockSpec(e_mat.shape, lambda b, g, i: (0, 0)),
            tab((NSA_HG * QB, 256)), tab((NSA_HG * QB, 640)),
            tab((NSA_HG * QB, 128)), tab((NSA_HG * QB, 128)),
        ],
        out_specs=pl.BlockSpec((QB, NSA_HG * DK), lambda b, g, i: (b * NQ + i, g)),
        out_shape=jax.ShapeDtypeStruct((T, NSA_HEADS * DK), BF16),
        scratch_shapes=[
            pltpu.VMEM((NSA_HG * QB, 256), F32),
            pltpu.VMEM((NSA_HG * QB, 1), F32),
            pltpu.VMEM((NSA_HG * QB, 1), F32),
            pltpu.VMEM((NSA_HG * QB, DK), F32),
        ],
        compiler_params=pltpu.CompilerParams(
            dimension_semantics=("parallel", "parallel", "arbitrary"), vmem_limit_bytes=VMEM_LIMIT),
        name="nsa_attention",
    )(proj, proj, proj, proj, proj, proj, kvc, kvc, msel, e_mat, bnear, wtab, tnear, cbias)


def _retention_kernel(q_ref, k_ref, v_ref, g_ref, cos_ref, sin_ref, dec_ref, xz_ref, gc_ref, o_ref, r_scr):
    @pl.when(pl.program_id(1) == 0)
    def _():
        r_scr[...] = jnp.zeros(r_scr.shape, F32)

    cos = cos_ref[...]
    sin = sin_ref[...]
    hd = RET_D // 2

    def rot(x):
        x1, x2 = x[:, :hd], x[:, hd:]
        return jnp.concatenate([x1 * cos - x2 * sin, x2 * cos + x1 * sin], axis=1)

    for h in range(RET_HEADS):
        sl = slice(h * RET_D, (h + 1) * RET_D)
        qh = rot(q_ref[:, sl].astype(F32))
        kh = rot(k_ref[:, sl].astype(F32))
        vh = v_ref[:, sl]
        xi = xz_ref[h, 0]
        zeta = xz_ref[h, 1]
        qb16 = qh.astype(BF16)
        inner = _dot_nt(qb16, kh.astype(BF16)) * dec_ref[h]
        o = _dot(inner.astype(BF16), vh)
        r_old = r_scr[h]
        cross = _dot(qb16, r_old.astype(BF16))
        o = o + cross * jnp.concatenate([xi, xi], axis=1)
        kz = (kh * jnp.concatenate([zeta, zeta], axis=1)).astype(BF16)
        r_scr[h] = r_old * gc_ref[h] + _dot(kz.T, vh)
        mu = jnp.mean(o, axis=-1, keepdims=True)
        var = jnp.mean(jnp.square(o - mu), axis=-1, keepdims=True)
        on = (o - mu) * lax.rsqrt(var + GN_EPS)
        gf = g_ref[:, sl].astype(F32)
        o_ref[:, sl] = (gf * jax.nn.sigmoid(gf) * on).astype(o_ref.dtype)


def _retention(proj, cos, sin, decay, xz, gc, B, S):
    T = B * S
    N = S // RET_CHUNK
    W = RET_HEADS * RET_D
    sec = lambda c0: pl.BlockSpec((RET_CHUNK, W), lambda b, n: (b * N + n, c0 // W))
    return pl.pallas_call(
        _retention_kernel,
        grid=(B, N),
        in_specs=[
            sec(COL_QR), sec(COL_KR), sec(COL_VR), sec(COL_GR),
            pl.BlockSpec((RET_CHUNK, RET_D // 2), lambda b, n: (n, 0)),
            pl.BlockSpec((RET_CHUNK, RET_D // 2), lambda b, n: (n, 0)),
            pl.BlockSpec((RET_HEADS, RET_CHUNK, RET_CHUNK), lambda b, n: (0, 0, 0)),
            pl.BlockSpec((RET_HEADS, 2, RET_CHUNK, 128), lambda b, n: (0, 0, 0, 0)),
            pl.BlockSpec(memory_space=pltpu.SMEM),
        ],
        out_specs=pl.BlockSpec((RET_CHUNK, W), lambda b, n: (b * N + n, 0)),
        out_shape=jax.ShapeDtypeStruct((T, W), BF16),
        scratch_shapes=[pltpu.VMEM((RET_HEADS, RET_D, RET_D), F32)],
        compiler_params=pltpu.CompilerParams(
            dimension_semantics=("parallel", "arbitrary"), vmem_limit_bytes=VMEM_LIMIT),
        name="retention",
    )(proj, proj, proj, proj, cos, sin, decay, xz, gc)


def _rms(v, g):
    return v * lax.rsqrt(jnp.mean(v * v, axis=-1, keepdims=True) + NORM_EPS) * g


def _outproj_kernel(oa_ref, or_ref, w_ref, x_ref, gpost_ref, gpre_ref, x1_ref, h2_ref):
    wa = oa_ref.shape[1]
    mix = _dot(oa_ref[...], w_ref[:wa, :]) + _dot(or_ref[...], w_ref[wa:, :])
    x1 = x_ref[...] + _rms(mix, gpost_ref[...])
    x1_ref[...] = x1
    h2_ref[...] = _rms(x1, gpre_ref[...]).astype(h2_ref.dtype)


def _out_proj(oa, orr, w_out, x2, g_post, g_pre, tm=512):
    T, D = x2.shape
    tm = min(tm, T)
    row = lambda w: pl.BlockSpec((tm, w), lambda i: (i, 0))
    full = lambda a: pl.BlockSpec(a.shape, lambda i: (0,) * a.ndim)
    return pl.pallas_call(
        _outproj_kernel,
        grid=(T // tm,),
        in_specs=[row(oa.shape[1]), row(orr.shape[1]), full(w_out), row(D), full(g_post), full(g_pre)],
        out_specs=[row(D), row(D)],
        out_shape=[jax.ShapeDtypeStruct((T, D), F32), jax.ShapeDtypeStruct((T, D), BF16)],
        compiler_params=pltpu.CompilerParams(
            dimension_semantics=("parallel",), vmem_limit_bytes=VMEM_LIMIT),
        name="out_proj",
    )(oa, orr, w_out, x2, g_post, g_pre)


def _mlp_kernel(h_ref, wu_ref, wd_ref, x1_ref, g_ref, o_ref, acc_ref):
    f = pl.program_id(1)

    @pl.when(f == 0)
    def _():
        acc_ref[...] = jnp.zeros(acc_ref.shape, F32)

    u = jnp.maximum(_dot(h_ref[...], wu_ref[...]), 0.0)
    acc_ref[...] += _dot((u * u).astype(BF16), wd_ref[...])

    @pl.when(f == pl.num_programs(1) - 1)
    def _():
        o_ref[...] = x1_ref[...] + _rms(acc_ref[...], g_ref[...])


def _mlp(h2, w_up, w_down, x1, g_post, tm=512, tf=512):
    T, D = x1.shape
    F = w_up.shape[1]
    tm = min(tm, T)
    return pl.pallas_call(
        _mlp_kernel,
        grid=(T // tm, F // tf),
        in_specs=[
            pl.BlockSpec((tm, D), lambda i, f: (i, 0)),
            pl.BlockSpec((D, tf), lambda i, f: (0, f)),
            pl.BlockSpec((tf, D), lambda i, f: (f, 0)),
            pl.BlockSpec((tm, D), lambda i, f: (i, 0)),
            pl.BlockSpec((1, D), lambda i, f: (0, 0)),
        ],
        out_specs=pl.BlockSpec((tm, D), lambda i, f: (i, 0)),
        out_shape=jax.ShapeDtypeStruct((T, D), F32),
        scratch_shapes=[pltpu.VMEM((tm, D), F32)],
        compiler_params=pltpu.CompilerParams(
            dimension_semantics=("parallel", "arbitrary"), vmem_limit_bytes=VMEM_LIMIT),
        name="mlp",
    )(h2, w_up, w_down, x1, g_post)


def _t5_bucket_np(rel):
    n = np.maximum(rel, 0)
    max_exact = T5_BUCKETS // 2
    nf = np.maximum(n, 1).astype(np.float64)
    large = max_exact + (np.log(nf / max_exact) / np.log(T5_MAX_DIST / max_exact)
                         * (T5_BUCKETS - max_exact)).astype(np.int64)
    large = np.minimum(large, T5_BUCKETS - 1)
    return np.where(n < max_exact, n, large)


@functools.lru_cache(maxsize=None)
def _static_tables(S):
    R = 1024
    bucket = _t5_bucket_np(np.arange(R))
    i = np.arange(QB)[:, None]
    kk = np.arange(256)[None, :]
    rel_near = i - kk + QB
    idx_near = np.where(rel_near >= 0, np.minimum(rel_near, R - 1), R)
    kk = np.arange(640)[None, :]
    rel_win = i - kk + WINDOW
    idx_win = np.where((rel_win >= 0) & (rel_win < WINDOW), np.minimum(rel_win, R - 1), R)
    m = np.arange(128)[None, :]
    rel_cmp = i - CMP_STRIDE * (m - CMP_PAD) - (CMP_BLOCK - 1)
    idx_cmp = np.where(rel_cmp >= 0, np.minimum(rel_cmp, R - 1), R)
    n_cmp = (S - CMP_BLOCK) // CMP_STRIDE + 1
    n_sel = S // SEL_BLOCK
    cs = np.arange(n_cmp) * CMP_STRIDE
    ss = np.arange(n_sel) * SEL_BLOCK
    overlap = (cs[:, None] <= ss[None, :] + SEL_BLOCK - 1) & (cs[:, None] + CMP_BLOCK - 1 >= ss[None, :])
    rows = CMP_PAD + S // CMP_STRIDE + 8
    msel = np.zeros((rows, 128), np.float32)
    msel[CMP_PAD:CMP_PAD + n_cmp, :n_sel] = overlap
    e_mat = (np.arange(S)[:, None] // SEL_BLOCK == np.arange(128)[None, :]).astype(np.float32)
    log_gamma = np.log(1.0 - np.exp2(-5.0 - np.arange(RET_HEADS, dtype=np.float32))).astype(np.float32)
    idx = np.arange(RET_CHUNK, dtype=np.float32)
    diff = idx[:, None] - idx[None, :]
    decay = np.where(diff[None] >= 0, np.exp(np.maximum(diff, 0.0)[None] * log_gamma[:, None, None]), 0.0)
    xi = np.exp((idx + 1.0)[None, :] * log_gamma[:, None])
    zeta = np.exp((RET_CHUNK - 1.0 - idx)[None, :] * log_gamma[:, None])
    xz = np.stack([np.broadcast_to(xi[:, :, None], (RET_HEADS, RET_CHUNK, 128)),
                   np.broadcast_to(zeta[:, :, None], (RET_HEADS, RET_CHUNK, 128))], axis=1)
    g_c = np.exp(RET_CHUNK * log_gamma)
    inv_freq = ROPE_BASE ** (-np.arange(0, RET_D, 2, dtype=np.float32) / RET_D)
    ang = np.arange(S, dtype=np.float32)[:, None] * inv_freq[None, :]
    return dict(bucket=bucket, idx_near=idx_near, idx_win=idx_win, idx_cmp=idx_cmp,
                msel=msel, e_mat=e_mat, decay=decay.astype(np.float32), xz=xz.astype(np.float32),
                g_c=g_c.astype(np.float32), ang=ang.astype(np.float32), R=R)


def _bias_tables(t5_bias, st):
    R = st["R"]
    rel_bias = jnp.take(t5_bias.astype(F32), jnp.asarray(st["bucket"]), axis=0)
    rel_bias = jnp.concatenate([rel_bias, jnp.full((1, NSA_HEADS), NEG, F32)], axis=0).T

    def table(idx):
        t = jnp.take(rel_bias, jnp.asarray(idx), axis=1)
        return t.reshape(NSA_G, NSA_HG * QB, idx.shape[1])

    cfar = jnp.broadcast_to(rel_bias[:, R - 1][:, None, None], (NSA_HEADS, QB, 128))
    return (table(st["idx_near"]), table(st["idx_win"]), table(st["idx_cmp"]),
            cfar.reshape(NSA_G, NSA_HG * QB, 128))


def kernel(x, norm_mix_pre, w_in, cmp_pe_k, cmp_w1_k, cmp_b1_k, cmp_w2_k, cmp_pe_v, cmp_w1_v, cmp_b1_v,
           cmp_w2_v, t5_bias, w_out, norm_mix_post, norm_mlp_pre, w_up, w_down, norm_mlp_post):
    B, S, D = x.shape
    T = B * S
    depth = w_in.shape[0]
    st = _static_tables(S)
    bnear, wtab, tnear, cbias = _bias_tables(t5_bias, st)
    msel = jnp.asarray(st["msel"])
    e_mat = jnp.asarray(st["e_mat"], BF16)
    decay = jnp.asarray(st["decay"])
    xz = jnp.asarray(st["xz"])
    gc = jnp.asarray(st["g_c"])
    ang = jnp.asarray(st["ang"])
    cos, sin = jnp.cos(ang), jnp.sin(ang)

    colscale = np.ones((1, N_PROJ), np.float32)
    colscale[0, COL_QA:COL_QA + NSA_HEADS * DK] = DK ** -0.5
    colscale[0, COL_KR:COL_KR + RET_HEADS * RET_D] = RET_D ** -0.5
    colscale = jnp.asarray(colscale)
    kv_lo = NSA_HEADS * DK
    gate_lo = kv_lo + 6 * NSA_G * DK
    gate_mid = gate_lo + 3 * NSA_HG
    gate_hi = gate_lo + 3 * NSA_HEADS

    xcur = x.reshape(T, D)
    for l in range(depth):
        w = w_in[l]
        w_re = jnp.concatenate(
            [w[:, :kv_lo], w[:, gate_hi:], w[:, kv_lo:gate_lo],
             w[:, gate_lo:gate_mid], jnp.zeros((D, 128 - 3 * NSA_HG), w.dtype),
             w[:, gate_mid:gate_hi], jnp.zeros((D, N_PROJ - COL_GATE - 128 - 3 * NSA_HG), w.dtype)],
            axis=1).astype(BF16)
        proj = _in_proj(xcur, norm_mix_pre[l][None, :], w_re, colscale)

        NC = S // CMP_STRIDE
        xc = proj[:, COL_KV:COL_KV + 4 * DK].reshape(B, NC, CMP_STRIDE, 4, DK)
        xc = xc.transpose(0, 3, 1, 2, 4).reshape(B, 4, NC, CMP_STRIDE * DK)
        half = CMP_STRIDE * DK
        pe = jnp.stack([cmp_pe_k[l].reshape(2, half), cmp_pe_v[l].reshape(2, half)])
        pe = jnp.concatenate([pe, jnp.zeros((2, 6, half), F32)], axis=1)
        w1 = jnp.stack([cmp_w1_k[l], cmp_w1_v[l]]).astype(BF16)
        b1 = jnp.stack([cmp_b1_k[l], cmp_b1_v[l]])[:, None, :]
        w2 = jnp.stack([cmp_w2_k[l], cmp_w2_v[l]]).astype(BF16)
        kvc = _compress(xc, pe, w1, b1, w2)

        o_a = _nsa(proj, kvc, msel, e_mat, bnear, wtab, tnear, cbias, B, S)

        o_r = _retention(proj, cos, sin, decay, xz, gc, B, S)

        x1, h2 = _out_proj(o_a, o_r, w_out[l].astype(BF16), xcur,
                           norm_mix_post[l][None, :], norm_mlp_pre[l][None, :])
        xcur = _mlp(h2, w_up[l].astype(BF16), w_down[l].astype(BF16), x1, norm_mlp_post[l][None, :])
    return xcur.reshape(B, S, D)
```

```python
import functools

import numpy as np
import jax
import jax.numpy as jnp
from jax import lax
from jax.experimental import pallas as pl
from jax.experimental.pallas import tpu as pltpu

F32 = jnp.float32
BF16 = jnp.bfloat16

D_MODEL = 2048
NSA_HEADS = 8
NSA_G = 2
NSA_HG = 4
DK = 128
CMP_BLOCK = 32
CMP_STRIDE = 16
SEL_BLOCK = 64
SEL_TOPN = 16
WINDOW = 512
QB = 128
RET_HEADS = 4
RET_D = 256
RET_CHUNK = 128
ROPE_BASE = 10000.0
GN_EPS = 1e-6
NORM_EPS = 1e-6
T5_BUCKETS = 32
T5_MAX_DIST = 128
D_FF = 4 * D_MODEL
FORCE = 1e4
NEG = -1e30
M_FLOOR = -1e20

COL_QA = 0
COL_QR = 1024
COL_KR = 2048
COL_VR = 3072
COL_GR = 4096
COL_KV = 5120
COL_GATE = 6656
N_PROJ = 7168
CMP_PAD = 120

VMEM_LIMIT = 56 * 1024 * 1024


def _dot(a, b):
    return jnp.dot(a, b, preferred_element_type=F32)


def _dot_nt(a, b):
    return lax.dot_general(a, b, (((1,), (1,)), ((), ())), preferred_element_type=F32)


def _inproj_kernel(x_ref, g_ref, w_ref, cs_ref, o_ref, h_scr):
    @pl.when(pl.program_id(1) == 0)
    def _():
        xf = x_ref[...]
        ms = jnp.mean(xf * xf, axis=-1, keepdims=True)
        h_scr[...] = (xf * lax.rsqrt(ms + NORM_EPS) * g_ref[...]).astype(BF16)

    acc = _dot(h_scr[...], w_ref[...])
    o_ref[...] = (acc * cs_ref[...]).astype(o_ref.dtype)


def _in_proj(x2, gain, w, colscale, tm=1024, tn=512):
    T, D = x2.shape
    N = w.shape[1]
    tm = min(tm, T)
    return pl.pallas_call(
        _inproj_kernel,
        grid=(T // tm, N // tn),
        in_specs=[
            pl.BlockSpec((tm, D), lambda i, j: (i, 0)),
            pl.BlockSpec((1, D), lambda i, j: (0, 0)),
            pl.BlockSpec((D, tn), lambda i, j: (0, j)),
            pl.BlockSpec((1, tn), lambda i, j: (0, j)),
        ],
        out_specs=pl.BlockSpec((tm, tn), lambda i, j: (i, j)),
        out_shape=jax.ShapeDtypeStruct((T, N), BF16),
        scratch_shapes=[pltpu.VMEM((tm, D), BF16)],
        compiler_params=pltpu.CompilerParams(
            dimension_semantics=("parallel", "arbitrary"), vmem_limit_bytes=VMEM_LIMIT),
        name="in_proj",
    )(x2, gain, w, colscale)


def _compress_kernel(x_ref, pe_ref, w1_ref, b1_ref, w2_ref, o_ref):
    nc = x_ref.shape[2]
    half = x_ref.shape[3]
    xf = x_ref[0, 0].astype(F32)
    xa = (xf + pe_ref[0, 0:1, :]).astype(BF16)
    xb = (xf + pe_ref[0, 1:2, :]).astype(BF16)
    a = _dot(xa, w1_ref[0, :half, :])
    b = _dot(xb, w1_ref[0, half:, :])
    b_next = pltpu.roll(b, nc - 1, 0)
    hid = jax.nn.gelu(a + b_next + b1_ref[0])
    out = _dot(hid.astype(BF16), w2_ref[0])
    o_ref[0, 0, :CMP_PAD, :] = jnp.zeros((CMP_PAD, DK), F32)
    o_ref[0, 0, CMP_PAD:CMP_PAD + nc, :] = out
    o_ref[0, 0, CMP_PAD + nc:, :] = jnp.zeros((o_ref.shape[2] - CMP_PAD - nc, DK), F32)


def _compress(xc, pe, w1, b1, w2):
    B, _, NC, half = xc.shape
    rows = CMP_PAD + NC + 8
    return pl.pallas_call(
        _compress_kernel,
        grid=(B, 4),
        in_specs=[
            pl.BlockSpec((1, 1, NC, half), lambda b, s: (b, s, 0, 0)),
            pl.BlockSpec((1, 8, half), lambda b, s: (s // 2, 0, 0)),
            pl.BlockSpec((1, 2 * half, DK), lambda b, s: (s // 2, 0, 0)),
            pl.BlockSpec((1, 1, DK), lambda b, s: (s // 2, 0, 0)),
            pl.BlockSpec((1, DK, DK), lambda b, s: (s // 2, 0, 0)),
        ],
        out_specs=pl.BlockSpec((1, 1, rows, DK), lambda b, s: (b, s, 0, 0)),
        out_shape=jax.ShapeDtypeStruct((B, 4, rows, DK), F32),
        compiler_params=pltpu.CompilerParams(
            dimension_semantics=("parallel", "parallel"), vmem_limit_bytes=VMEM_LIMIT),
        name="compress_kv",
    )(xc, pe, w1, b1, w2)


TAB_NEAR, TAB_WIN, TAB_CMP, TAB_CONST, TAB_ROWS = 0, 256, 896, 1024, 1152
ONES_ROWS = 16


def _nsa_kernel(q_ref, ksl_ref, vslt_ref, kw_ref, vwt_ref, gate_ref, kc_ref, vc_ref,
                msel_ref, e_ref, tab_ref, o_ref, acc_ref, m_ref, l_ref, out_ref,
                s0_ref, s1_ref, p0_ref, p1_ref, a0_ref, a1_ref):
    qb = pl.program_id(2)
    cols = NSA_HG * QB
    qf = q_ref[...].astype(F32)
    qs_t = jnp.concatenate([qf[:, h * DK:(h + 1) * DK].T for h in range(NSA_HG)], axis=1).astype(BF16)
    g_t = jax.nn.sigmoid(gate_ref[...].astype(F32)).T

    def gate_row(c):
        return jnp.concatenate([g_t[3 * h + c:3 * h + c + 1, :] for h in range(NSA_HG)], axis=1)

    sub = lax.broadcasted_iota(jnp.int32, (128, cols), 0)

    m_ref[...] = jnp.full((1, cols), M_FLOOR, F32)
    l_ref[...] = jnp.zeros((1, cols), F32)
    acc_ref[...] = jnp.zeros((256, cols), F32)

    def cmp_chunk(start, bias):
        kc = kc_ref[0, 0, pl.ds(start, 128), :].astype(BF16)
        vc_t = vc_ref[0, 0, pl.ds(start, 128), :].T
        ms_t = msel_ref[pl.ds(start, 128), :].T
        s = _dot(kc, qs_t) + bias
        m_old = m_ref[...]
        m_new = jnp.maximum(m_old, jnp.max(s, axis=0, keepdims=True))
        alpha = jnp.exp(m_old - m_new)
        p = jnp.exp(s - m_new)
        l_ref[...] = alpha * l_ref[...] + jnp.sum(p, axis=0, keepdims=True)
        lhs = jnp.concatenate([vc_t, ms_t], axis=0).astype(BF16)
        acc_ref[...] = alpha * acc_ref[...] + _dot(lhs, p.astype(BF16))
        m_ref[...] = m_new

    near_start = pl.multiple_of(8 * qb, 8)
    cmp_chunk(near_start, tab_ref[0, TAB_CMP:TAB_CMP + 128, :]
              + jnp.where(sub + (8 * qb - CMP_PAD) < 0, NEG, 0.0))

    def far_body(c, carry):
        st = pl.multiple_of(CMP_PAD + 128 * c, 8)
        cmp_chunk(st, tab_ref[0, TAB_CONST:TAB_CONST + 128, :]
                  + jnp.where(128 * c + sub < 8 * qb - CMP_PAD, 0.0, NEG))
        return carry

    lax.fori_loop(0, (8 * qb + 7) // 128, far_body, 0)

    l = l_ref[...]
    inv = jnp.where(l > 0.0, 1.0 / l, 0.0)
    accv = acc_ref[...]
    out_ref[...] = accv[:DK] * (gate_row(0) * inv)
    u = accv[DK:] * inv
    imp_t = u[:, 0:QB] + u[:, QB:2 * QB] + u[:, 2 * QB:3 * QB] + u[:, 3 * QB:4 * QB]

    ji = lax.broadcasted_iota(jnp.int32, (128, QB), 0)
    qi = lax.broadcasted_iota(jnp.int32, (128, QB), 1)
    cur = 2 * qb + (qi >= SEL_BLOCK).astype(jnp.int32)
    forced = (ji == 0) | (ji == cur) | (ji == cur - 1)
    vt0 = jnp.where(ji <= cur, jnp.where(forced, FORCE, imp_t), -1.0)
    vt = vt0
    jio = ji.astype(F32)
    sel_t = jnp.zeros((128, QB), F32)
    for _ in range(SEL_TOPN):
        mx = jnp.max(vt, axis=0, keepdims=True)
        idx = jnp.min(jnp.where(vt == mx, jio, 128.0), axis=0, keepdims=True)
        hit = jio == idx
        sel_t = jnp.where(hit, 1.0, sel_t)
        vt = jnp.where(hit, -jnp.inf, vt)
    sel_t = jnp.where(vt0 >= 0.0, sel_t, 0.0)
    sb_t = (sel_t - 1.0) * (-NEG)
    sb_far = jnp.where(ji < 2 * qb - 2, sb_t, NEG)
    aug_far = (jnp.concatenate([sb_far] * NSA_HG, axis=1) + tab_ref[0, TAB_CONST:TAB_CONST + 128, :]).astype(BF16)
    aug_near = jnp.concatenate([sb_t] * NSA_HG, axis=1).astype(BF16)
    q_far = jnp.concatenate([qs_t, aug_far], axis=0)
    q_near = jnp.concatenate([qs_t, aug_near], axis=0)

    m_ref[...] = jnp.full((1, cols), M_FLOOR, F32)
    acc_ref[...] = jnp.zeros((256, cols), F32)
    ones_t = jnp.ones((ONES_ROWS, 256), BF16)
    n_tiles = e_ref.shape[0] // 256
    arows = DK + ONES_ROWS

    def k_aug(t):
        st = pl.multiple_of(t * QB, QB)
        return jnp.concatenate([ksl_ref[pl.ds(st, QB), :], e_ref[pl.ds(st, QB), :]], axis=1)

    def scores(t):
        tt = jnp.clip(t, 0, n_tiles - 1)
        return _dot(jnp.concatenate([k_aug(2 * tt), k_aug(2 * tt + 1)], axis=0), q_far)

    def softmax_stage(s, p_buf, a_buf):
        m_old = m_ref[...]
        m_new = jnp.maximum(m_old, jnp.max(s, axis=0, keepdims=True))
        a_buf[...] = jnp.exp(m_old - m_new)
        p_buf[...] = jnp.exp(s - m_new).astype(BF16)
        m_ref[...] = m_new

    def values_stage(v_t, p_buf, a_buf):
        lhs = jnp.concatenate([v_t, ones_t], axis=0)
        acc_ref[:arows] = a_buf[...] * acc_ref[:arows] + _dot(lhs, p_buf[...])

    def v_far(t):
        tt = jnp.clip(t, 0, n_tiles - 1)
        return jnp.concatenate([vslt_ref[0, 0, 2 * tt], vslt_ref[0, 0, 2 * tt + 1]], axis=1)

    n_far = qb // 2
    s0_ref[...] = scores(0)
    p1_ref[...] = jnp.zeros(p1_ref.shape, BF16)
    a1_ref[...] = jnp.ones(a1_ref.shape, F32)

    def sel_far_body(v, carry):
        a = 2 * v
        s1_ref[...] = scores(a + 1)
        softmax_stage(s0_ref[...], p0_ref, a0_ref)
        values_stage(v_far(a - 1), p1_ref, a1_ref)
        s0_ref[...] = scores(a + 2)
        softmax_stage(s1_ref[...], p1_ref, a1_ref)
        values_stage(v_far(a), p0_ref, a0_ref)
        return carry

    n_pairs = (n_far + 1) // 2
    lax.fori_loop(0, n_pairs, sel_far_body, 0)
    values_stage(v_far(2 * n_pairs - 1), p1_ref, a1_ref)

    tp = jnp.maximum(qb - 1, 0)
    first_mask = jnp.where(qb == 0, NEG, 0.0)
    ka = jnp.concatenate([k_aug(tp), k_aug(qb)], axis=0)
    bias = jnp.concatenate([tab_ref[0, TAB_NEAR:TAB_NEAR + QB, :] + first_mask,
                            tab_ref[0, TAB_NEAR + QB:TAB_NEAR + 2 * QB, :]], axis=0)
    softmax_stage(_dot(ka, q_near) + bias, p0_ref, a0_ref)
    values_stage(jnp.concatenate([vslt_ref[0, 0, tp], vslt_ref[0, 0, qb]], axis=1), p0_ref, a0_ref)
    accv = acc_ref[:arows]
    out_ref[...] += gate_row(1) * (accv[:DK] / accv[DK:DK + 1])

    tiles = [jnp.maximum(qb + d, 0) for d in range(-4, 1)]
    kwin = jnp.concatenate([kw_ref[pl.ds(pl.multiple_of(t * QB, QB), QB), :] for t in tiles], axis=0)
    vwin_t = jnp.concatenate([vwt_ref[0, 0, t] for t in tiles], axis=1)
    lhs = jnp.concatenate([vwin_t, jnp.ones((ONES_ROWS, 5 * QB), BF16)], axis=0)
    bias = jnp.concatenate(
        [tab_ref[0, TAB_WIN + i * QB:TAB_WIN + (i + 1) * QB, :] + jnp.where(qb + d < 0, NEG, 0.0)
         for i, d in enumerate(range(-4, 1))], axis=0)
    s = _dot(kwin, qs_t) + bias
    p = jnp.exp(s - jnp.max(s, axis=0, keepdims=True))
    r = _dot(lhs, p.astype(BF16))
    res = out_ref[...] + gate_row(2) * (r[:DK] / r[DK:DK + 1])
    for h in range(NSA_HG):
        o_ref[:, h * DK:(h + 1) * DK] = res[:, h * QB:(h + 1) * QB].T.astype(o_ref.dtype)


def _nsa(proj, vslt, vwt, kvc, msel, e_mat, tab, B, S):
    T = B * S
    NQ = S // QB
    kv0 = COL_KV // DK
    rows_c = kvc.shape[2]
    seq = lambda off: pl.BlockSpec((S, DK), lambda b, g, i: (b, off + g))
    vt_spec = pl.BlockSpec((1, 1, NQ, DK, QB), lambda b, g, i: (b, g, 0, 0, 0))
    return pl.pallas_call(
        _nsa_kernel,
        grid=(B, NSA_G, NQ),
        in_specs=[
            pl.BlockSpec((QB, NSA_HG * DK), lambda b, g, i: (b * NQ + i, g)),
            seq(kv0 + 4), vt_spec, seq(kv0 + 8), vt_spec,
            pl.BlockSpec((QB, 128), lambda b, g, i: (b * NQ + i, COL_GATE // 128 + g)),
            pl.BlockSpec((1, 1, rows_c, DK), lambda b, g, i: (b, g, 0, 0)),
            pl.BlockSpec((1, 1, rows_c, DK), lambda b, g, i: (b, 2 + g, 0, 0)),
            pl.BlockSpec(msel.shape, lambda b, g, i: (0, 0)),
            pl.BlockSpec(e_mat.shape, lambda b, g, i: (0, 0)),
            pl.BlockSpec((1, TAB_ROWS, NSA_HG * QB), lambda b, g, i: (g, 0, 0)),
        ],
        out_specs=pl.BlockSpec((QB, NSA_HG * DK), lambda b, g, i: (b * NQ + i, g)),
        out_shape=jax.ShapeDtypeStruct((T, NSA_HEADS * DK), BF16),
        scratch_shapes=[
            pltpu.VMEM((256, NSA_HG * QB), F32),
            pltpu.VMEM((1, NSA_HG * QB), F32),
            pltpu.VMEM((1, NSA_HG * QB), F32),
            pltpu.VMEM((DK, NSA_HG * QB), F32),
            pltpu.VMEM((256, NSA_HG * QB), F32), pltpu.VMEM((256, NSA_HG * QB), F32),
            pltpu.VMEM((256, NSA_HG * QB), BF16), pltpu.VMEM((256, NSA_HG * QB), BF16),
            pltpu.VMEM((1, NSA_HG * QB), F32), pltpu.VMEM((1, NSA_HG * QB), F32),
        ],
        compiler_params=pltpu.CompilerParams(
            dimension_semantics=("parallel", "parallel", "arbitrary"), vmem_limit_bytes=VMEM_LIMIT),
        name="nsa_attention",
    )(proj, proj, vslt, proj, vwt, proj, kvc, kvc, msel, e_mat, tab)


def _retention_kernel(q_ref, k_ref, v_ref, g_ref, cos_ref, sin_ref, dec_ref, xz_ref, gc_ref, o_ref, r_scr):
    @pl.when(pl.program_id(1) == 0)
    def _():
        r_scr[...] = jnp.zeros(r_scr.shape, F32)

    cos = cos_ref[...]
    sin = sin_ref[...]
    hd = RET_D // 2

    def rot(x):
        x1, x2 = x[:, :hd], x[:, hd:]
        return jnp.concatenate([x1 * cos - x2 * sin, x2 * cos + x1 * sin], axis=1)

    for h in range(RET_HEADS):
        sl = slice(h * RET_D, (h + 1) * RET_D)
        qh = rot(q_ref[:, sl].astype(F32))
        kh = rot(k_ref[:, sl].astype(F32))
        vh = v_ref[:, sl]
        xi = xz_ref[h, 0]
        zeta = xz_ref[h, 1]
        qb16 = qh.astype(BF16)
        inner = _dot_nt(qb16, kh.astype(BF16)) * dec_ref[h]
        o = _dot(inner.astype(BF16), vh)
        r_old = r_scr[h]
        cross = _dot(qb16, r_old.astype(BF16))
        o = o + cross * jnp.concatenate([xi, xi], axis=1)
        kz = (kh * jnp.concatenate([zeta, zeta], axis=1)).astype(BF16)
        r_scr[h] = r_old * gc_ref[h] + _dot(kz.T, vh)
        mu = jnp.mean(o, axis=-1, keepdims=True)
        var = jnp.mean(jnp.square(o - mu), axis=-1, keepdims=True)
        on = (o - mu) * lax.rsqrt(var + GN_EPS)
        gf = g_ref[:, sl].astype(F32)
        o_ref[:, sl] = (gf * jax.nn.sigmoid(gf) * on).astype(o_ref.dtype)


def _retention(proj, cos, sin, decay, xz, gc, B, S):
    T = B * S
    N = S // RET_CHUNK
    W = RET_HEADS * RET_D
    sec = lambda c0: pl.BlockSpec((RET_CHUNK, W), lambda b, n: (b * N + n, c0 // W))
    return pl.pallas_call(
        _retention_kernel,
        grid=(B, N),
        in_specs=[
            sec(COL_QR), sec(COL_KR), sec(COL_VR), sec(COL_GR),
            pl.BlockSpec((RET_CHUNK, RET_D // 2), lambda b, n: (n, 0)),
            pl.BlockSpec((RET_CHUNK, RET_D // 2), lambda b, n: (n, 0)),
            pl.BlockSpec((RET_HEADS, RET_CHUNK, RET_CHUNK), lambda b, n: (0, 0, 0)),
            pl.BlockSpec((RET_HEADS, 2, RET_CHUNK, 128), lambda b, n: (0, 0, 0, 0)),
            pl.BlockSpec(memory_space=pltpu.SMEM),
        ],
        out_specs=pl.BlockSpec((RET_CHUNK, W), lambda b, n: (b * N + n, 0)),
        out_shape=jax.ShapeDtypeStruct((T, W), BF16),
        scratch_shapes=[pltpu.VMEM((RET_HEADS, RET_D, RET_D), F32)],
        compiler_params=pltpu.CompilerParams(
            dimension_semantics=("parallel", "arbitrary"), vmem_limit_bytes=VMEM_LIMIT),
        name="retention",
    )(proj, proj, proj, proj, cos, sin, decay, xz, gc)


def _rms(v, g):
    return v * lax.rsqrt(jnp.mean(v * v, axis=-1, keepdims=True) + NORM_EPS) * g


def _outproj_kernel(oa_ref, or_ref, w_ref, x_ref, gpost_ref, gpre_ref, x1_ref, h2_ref):
    wa = oa_ref.shape[1]
    mix = _dot(oa_ref[...], w_ref[:wa, :]) + _dot(or_ref[...], w_ref[wa:, :])
    x1 = x_ref[...] + _rms(mix, gpost_ref[...])
    x1_ref[...] = x1
    h2_ref[...] = _rms(x1, gpre_ref[...]).astype(h2_ref.dtype)


def _out_proj(oa, orr, w_out, x2, g_post, g_pre, tm=512):
    T, D = x2.shape
    tm = min(tm, T)
    row = lambda w: pl.BlockSpec((tm, w), lambda i: (i, 0))
    full = lambda a: pl.BlockSpec(a.shape, lambda i: (0,) * a.ndim)
    return pl.pallas_call(
        _outproj_kernel,
        grid=(T // tm,),
        in_specs=[row(oa.shape[1]), row(orr.shape[1]), full(w_out), row(D), full(g_post), full(g_pre)],
        out_specs=[row(D), row(D)],
        out_shape=[jax.ShapeDtypeStruct((T, D), F32), jax.ShapeDtypeStruct((T, D), BF16)],
        compiler_params=pltpu.CompilerParams(
            dimension_semantics=("parallel",), vmem_limit_bytes=VMEM_LIMIT),
        name="out_proj",
    )(oa, orr, w_out, x2, g_post, g_pre)


def _mlp_kernel(h_ref, wu_ref, wd_ref, x1_ref, g_ref, o_ref, acc_ref):
    f = pl.program_id(1)

    @pl.when(f == 0)
    def _():
        acc_ref[...] = jnp.zeros(acc_ref.shape, F32)

    u = jnp.maximum(_dot(h_ref[...], wu_ref[...]), 0.0)
    acc_ref[...] += _dot((u * u).astype(BF16), wd_ref[...])

    @pl.when(f == pl.num_programs(1) - 1)
    def _():
        o_ref[...] = x1_ref[...] + _rms(acc_ref[...], g_ref[...])


def _mlp(h2, w_up, w_down, x1, g_post, tm=512, tf=512):
    T, D = x1.shape
    F = w_up.shape[1]
    tm = min(tm, T)
    return pl.pallas_call(
        _mlp_kernel,
        grid=(T // tm, F // tf),
        in_specs=[
            pl.BlockSpec((tm, D), lambda i, f: (i, 0)),
            pl.BlockSpec((D, tf), lambda i, f: (0, f)),
            pl.BlockSpec((tf, D), lambda i, f: (f, 0)),
            pl.BlockSpec((tm, D), lambda i, f: (i, 0)),
            pl.BlockSpec((1, D), lambda i, f: (0, 0)),
        ],
        out_specs=pl.BlockSpec((tm, D), lambda i, f: (i, 0)),
        out_shape=jax.ShapeDtypeStruct((T, D), F32),
        scratch_shapes=[pltpu.VMEM((tm, D), F32)],
        compiler_params=pltpu.CompilerParams(
            dimension_semantics=("parallel", "arbitrary"), vmem_limit_bytes=VMEM_LIMIT),
        name="mlp",
    )(h2, w_up, w_down, x1, g_post)


def _t5_bucket_np(rel):
    n = np.maximum(rel, 0)
    max_exact = T5_BUCKETS // 2
    nf = np.maximum(n, 1).astype(np.float64)
    large = max_exact + (np.log(nf / max_exact) / np.log(T5_MAX_DIST / max_exact)
                         * (T5_BUCKETS - max_exact)).astype(np.int64)
    large = np.minimum(large, T5_BUCKETS - 1)
    return np.where(n < max_exact, n, large)


@functools.lru_cache(maxsize=None)
def _static_tables(S):
    masked = T5_BUCKETS
    i = np.arange(QB)[None, :]

    def bucket_rows(rel, valid):
        return np.where(valid, _t5_bucket_np(rel), masked)

    kk = np.arange(256)[:, None]
    rel = i - kk + QB
    b_near = bucket_rows(rel, rel >= 0)
    kk = np.arange(640)[:, None]
    rel = i - kk + WINDOW
    b_win = bucket_rows(rel, (rel >= 0) & (rel < WINDOW))
    m = np.arange(128)[:, None]
    rel = i - CMP_STRIDE * (m - CMP_PAD) - (CMP_BLOCK - 1)
    b_cmp = bucket_rows(rel, rel >= 0)
    b_const = np.full((128, QB), T5_BUCKETS - 1)
    bucket_idx = np.concatenate([b_near, b_win, b_cmp, b_const], axis=0).astype(np.int32)
    n_cmp = (S - CMP_BLOCK) // CMP_STRIDE + 1
    n_sel = S // SEL_BLOCK
    cs = np.arange(n_cmp) * CMP_STRIDE
    ss = np.arange(n_sel) * SEL_BLOCK
    overlap = (cs[:, None] <= ss[None, :] + SEL_BLOCK - 1) & (cs[:, None] + CMP_BLOCK - 1 >= ss[None, :])
    rows = CMP_PAD + S // CMP_STRIDE + 8
    msel = np.zeros((rows, 128), np.float32)
    msel[CMP_PAD:CMP_PAD + n_cmp, :n_sel] = overlap
    e_mat = (np.arange(S)[:, None] // SEL_BLOCK == np.arange(128)[None, :]).astype(np.float32)
    log_gamma = np.log(1.0 - np.exp2(-5.0 - np.arange(RET_HEADS, dtype=np.float32))).astype(np.float32)
    idx = np.arange(RET_CHUNK, dtype=np.float32)
    diff = idx[:, None] - idx[None, :]
    decay = np.where(diff[None] >= 0, np.exp(np.maximum(diff, 0.0)[None] * log_gamma[:, None, None]), 0.0)
    xi = np.exp((idx + 1.0)[None, :] * log_gamma[:, None])
    zeta = np.exp((RET_CHUNK - 1.0 - idx)[None, :] * log_gamma[:, None])
    xz = np.stack([np.broadcast_to(xi[:, :, None], (RET_HEADS, RET_CHUNK, 128)),
                   np.broadcast_to(zeta[:, :, None], (RET_HEADS, RET_CHUNK, 128))], axis=1)
    g_c = np.exp(RET_CHUNK * log_gamma)
    inv_freq = ROPE_BASE ** (-np.arange(0, RET_D, 2, dtype=np.float32) / RET_D)
    ang = np.arange(S, dtype=np.float32)[:, None] * inv_freq[None, :]
    return dict(bucket_idx=bucket_idx, msel=msel, e_mat=e_mat, decay=decay.astype(np.float32),
                xz=xz.astype(np.float32), g_c=g_c.astype(np.float32), ang=ang.astype(np.float32))


def _bias_table(t5_bias, st):
    assert st["bucket_idx"].shape == (TAB_ROWS, QB)
    idx = jnp.asarray(st["bucket_idx"].reshape(-1))
    onehot = (idx[None, :] == jnp.arange(T5_BUCKETS + 1, dtype=jnp.int32)[:, None]).astype(F32)
    vals = jnp.concatenate([t5_bias.astype(F32), jnp.full((1, NSA_HEADS), NEG, F32)], axis=0)
    tab = jnp.einsum("bh,bn->hn", vals, onehot, precision=lax.Precision.HIGHEST)
    tab = tab.reshape(NSA_G, NSA_HG, TAB_ROWS, QB).transpose(0, 2, 1, 3)
    return tab.reshape(NSA_G, TAB_ROWS, NSA_HG * QB)


def kernel(x, norm_mix_pre, w_in, cmp_pe_k, cmp_w1_k, cmp_b1_k, cmp_w2_k, cmp_pe_v, cmp_w1_v, cmp_b1_v,
           cmp_w2_v, t5_bias, w_out, norm_mix_post, norm_mlp_pre, w_up, w_down, norm_mlp_post):
    B, S, D = x.shape
    T = B * S
    depth = w_in.shape[0]
    st = _static_tables(S)
    tab = _bias_table(t5_bias, st)
    msel = jnp.asarray(st["msel"])
    e_mat = jnp.asarray(st["e_mat"], BF16)
    decay = jnp.asarray(st["decay"])
    xz = jnp.asarray(st["xz"])
    gc = jnp.asarray(st["g_c"])
    ang = jnp.asarray(st["ang"])
    cos, sin = jnp.cos(ang), jnp.sin(ang)

    colscale = np.ones((1, N_PROJ), np.float32)
    colscale[0, COL_QA:COL_QA + NSA_HEADS * DK] = DK ** -0.5
    colscale[0, COL_KR:COL_KR + RET_HEADS * RET_D] = RET_D ** -0.5
    colscale = jnp.asarray(colscale)
    kv_lo = NSA_HEADS * DK
    gate_lo = kv_lo + 6 * NSA_G * DK
    gate_mid = gate_lo + 3 * NSA_HG
    gate_hi = gate_lo + 3 * NSA_HEADS

    xcur = x.reshape(T, D)
    for l in range(depth):
        w = w_in[l]
        w_re = jnp.concatenate(
            [w[:, :kv_lo], w[:, gate_hi:], w[:, kv_lo:gate_lo],
             w[:, gate_lo:gate_mid], jnp.zeros((D, 128 - 3 * NSA_HG), w.dtype),
             w[:, gate_mid:gate_hi], jnp.zeros((D, N_PROJ - COL_GATE - 128 - 3 * NSA_HG), w.dtype)],
            axis=1).astype(BF16)
        proj = _in_proj(xcur, norm_mix_pre[l][None, :], w_re, colscale)

        NC = S // CMP_STRIDE
        xc = proj[:, COL_KV:COL_KV + 4 * DK].reshape(B, NC, CMP_STRIDE, 4, DK)
        xc = xc.transpose(0, 3, 1, 2, 4).reshape(B, 4, NC, CMP_STRIDE * DK)
        half = CMP_STRIDE * DK
        pe = jnp.stack([cmp_pe_k[l].reshape(2, half), cmp_pe_v[l].reshape(2, half)])
        pe = jnp.concatenate([pe, jnp.zeros((2, 6, half), F32)], axis=1)
        w1 = jnp.stack([cmp_w1_k[l], cmp_w1_v[l]]).astype(BF16)
        b1 = jnp.stack([cmp_b1_k[l], cmp_b1_v[l]])[:, None, :]
        w2 = jnp.stack([cmp_w2_k[l], cmp_w2_v[l]]).astype(BF16)
        kvc = _compress(xc, pe, w1, b1, w2)

        def v_tiles(c0):
            v = proj[:, c0:c0 + NSA_G * DK].reshape(B, S // QB, QB, NSA_G, DK)
            return v.transpose(0, 3, 1, 4, 2)

        o_a = _nsa(proj, v_tiles(COL_KV + 6 * DK), v_tiles(COL_KV + 10 * DK), kvc, msel, e_mat, tab, B, S)
        o_r = _retention(proj, cos, sin, decay, xz, gc, B, S)

        x1, h2 = _out_proj(o_a, o_r, w_out[l].astype(BF16), xcur,
                           norm_mix_post[l][None, :], norm_mlp_pre[l][None, :])
        xcur = _mlp(h2, w_up[l].astype(BF16), w_down[l].astype(BF16), x1, norm_mlp_post[l][None, :])
    return xcur.reshape(B, S, D)
```

```python
import functools

import numpy as np
import jax
import jax.numpy as jnp
from jax import lax
from jax.experimental import pallas as pl
from jax.experimental.pallas import tpu as pltpu

F32 = jnp.float32
BF16 = jnp.bfloat16

D_MODEL = 2048
NSA_HEADS = 8
NSA_G = 2
NSA_HG = 4
DK = 128
CMP_BLOCK = 32
CMP_STRIDE = 16
SEL_BLOCK = 64
SEL_TOPN = 16
WINDOW = 512
QB = 128
RET_HEADS = 4
RET_D = 256
RET_CHUNK = 128
ROPE_BASE = 10000.0
GN_EPS = 1e-6
NORM_EPS = 1e-6
T5_BUCKETS = 32
T5_MAX_DIST = 128
D_FF = 4 * D_MODEL
FORCE = 1e4
NEG = -1e30
M_FLOOR = -1e20
LOG2E = 1.4426950408889634

COL_QA = 0
COL_QR = 1024
COL_KR = 2048
COL_VR = 3072
COL_GR = 4096
COL_KV = 5120
COL_GATE = 6656
N_PROJ = 7168
CMP_PAD = 120

VMEM_LIMIT = 56 * 1024 * 1024


def _dot(a, b):
    return jnp.dot(a, b, preferred_element_type=F32)


def _dot_nt(a, b):
    return lax.dot_general(a, b, (((1,), (1,)), ((), ())), preferred_element_type=F32)


def _inproj_kernel(x_ref, g_ref, w_ref, cs_ref, o_ref, h_scr):
    @pl.when(pl.program_id(1) == 0)
    def _():
        xf = x_ref[...]
        ms = jnp.mean(xf * xf, axis=-1, keepdims=True)
        h_scr[...] = (xf * lax.rsqrt(ms + NORM_EPS) * g_ref[...]).astype(BF16)

    acc = _dot(h_scr[...], w_ref[...])
    o_ref[...] = (acc * cs_ref[...]).astype(o_ref.dtype)


def _in_proj(x2, gain, w, colscale, tm=1024, tn=512):
    T, D = x2.shape
    N = w.shape[1]
    tm = min(tm, T)
    return pl.pallas_call(
        _inproj_kernel,
        grid=(T // tm, N // tn),
        in_specs=[
            pl.BlockSpec((tm, D), lambda i, j: (i, 0)),
            pl.BlockSpec((1, D), lambda i, j: (0, 0)),
            pl.BlockSpec((D, tn), lambda i, j: (0, j)),
            pl.BlockSpec((1, tn), lambda i, j: (0, j)),
        ],
        out_specs=pl.BlockSpec((tm, tn), lambda i, j: (i, j)),
        out_shape=jax.ShapeDtypeStruct((T, N), BF16),
        scratch_shapes=[pltpu.VMEM((tm, D), BF16)],
        compiler_params=pltpu.CompilerParams(
            dimension_semantics=("parallel", "arbitrary"), vmem_limit_bytes=VMEM_LIMIT),
        name="in_proj",
    )(x2, gain, w, colscale)


def _relayout_kernel(cmp_ref, vs_ref, vw_ref, xc_ref, vt_ref, t_scr):
    tm = cmp_ref.shape[0]
    for s in range(4):
        t_scr[s] = cmp_ref[:, s * DK:(s + 1) * DK].astype(F32)
        for t in range(CMP_STRIDE):
            xc_ref[0, s, :, t * DK:(t + 1) * DK] = t_scr[
                s, pl.ds(t, tm // CMP_STRIDE, stride=CMP_STRIDE), :].astype(BF16)
    for kind, ref in enumerate((vs_ref, vw_ref)):
        v = ref[...].astype(F32)
        for g in range(NSA_G):
            for r in range(tm // QB):
                vt_ref[kind, 0, g, r] = v[r * QB:(r + 1) * QB, g * DK:(g + 1) * DK].T.astype(BF16)


def _relayout(proj, B, S, tm=1024):
    tm = min(tm, S)
    rpb = S // tm
    NC, NQ = S // CMP_STRIDE, S // QB
    return pl.pallas_call(
        _relayout_kernel,
        grid=(B * rpb,),
        in_specs=[
            pl.BlockSpec((tm, 4 * DK), lambda i: (i, COL_KV // (4 * DK))),
            pl.BlockSpec((tm, 2 * DK), lambda i: (i, (COL_KV + 6 * DK) // (2 * DK))),
            pl.BlockSpec((tm, 2 * DK), lambda i: (i, (COL_KV + 10 * DK) // (2 * DK))),
        ],
        out_specs=[
            pl.BlockSpec((1, 4, tm // CMP_STRIDE, CMP_STRIDE * DK), lambda i: (i // rpb, 0, i % rpb, 0)),
            pl.BlockSpec((2, 1, NSA_G, tm // QB, DK, QB), lambda i: (0, i // rpb, 0, i % rpb, 0, 0)),
        ],
        out_shape=[jax.ShapeDtypeStruct((B, 4, NC, CMP_STRIDE * DK), BF16),
                   jax.ShapeDtypeStruct((2, B, NSA_G, NQ, DK, QB), BF16)],
        scratch_shapes=[pltpu.VMEM((4, tm, DK), F32)],
        compiler_params=pltpu.CompilerParams(
            dimension_semantics=("parallel",), vmem_limit_bytes=VMEM_LIMIT),
        name="kv_relayout",
    )(proj, proj, proj)


def _compress_kernel(x_ref, pe_ref, w1_ref, b1_ref, w2_ref, o_ref):
    nc = x_ref.shape[2]
    half = x_ref.shape[3]
    xf = x_ref[0, 0].astype(F32)
    xa = (xf + pe_ref[0, 0:1, :]).astype(BF16)
    xb = (xf + pe_ref[0, 1:2, :]).astype(BF16)
    a = _dot(xa, w1_ref[0, :half, :])
    b = _dot(xb, w1_ref[0, half:, :])
    b_next = pltpu.roll(b, nc - 1, 0)
    hid = jax.nn.gelu(a + b_next + b1_ref[0])
    out = _dot(hid.astype(BF16), w2_ref[0])
    o_ref[0, 0, :CMP_PAD, :] = jnp.zeros((CMP_PAD, DK), F32)
    o_ref[0, 0, CMP_PAD:CMP_PAD + nc, :] = out
    o_ref[0, 0, CMP_PAD + nc:, :] = jnp.zeros((o_ref.shape[2] - CMP_PAD - nc, DK), F32)


def _compress(xc, pe, w1, b1, w2):
    B, _, NC, half = xc.shape
    rows = CMP_PAD + NC + 8
    return pl.pallas_call(
        _compress_kernel,
        grid=(B, 4),
        in_specs=[
            pl.BlockSpec((1, 1, NC, half), lambda b, s: (b, s, 0, 0)),
            pl.BlockSpec((1, 8, half), lambda b, s: (s // 2, 0, 0)),
            pl.BlockSpec((1, 2 * half, DK), lambda b, s: (s // 2, 0, 0)),
            pl.BlockSpec((1, 1, DK), lambda b, s: (s // 2, 0, 0)),
            pl.BlockSpec((1, DK, DK), lambda b, s: (s // 2, 0, 0)),
        ],
        out_specs=pl.BlockSpec((1, 1, rows, DK), lambda b, s: (b, s, 0, 0)),
        out_shape=jax.ShapeDtypeStruct((B, 4, rows, DK), F32),
        compiler_params=pltpu.CompilerParams(
            dimension_semantics=("parallel", "parallel"), vmem_limit_bytes=VMEM_LIMIT),
        name="compress_kv",
    )(xc, pe, w1, b1, w2)


TAB_NEAR, TAB_WIN, TAB_CMP, TAB_CONST, TAB_ROWS = 0, 256, 896, 1024, 1152
ONES_ROWS = 16
SEL_UNROLL = 4


def _nsa_kernel(q_ref, ksl_ref, vslt_ref, kw_ref, vwt_ref, gate_ref, kc_ref, vc_ref,
                msel_ref, e_ref, tab_ref, o_ref, acc_ref, m_ref, l_ref, out_ref,
                s0_ref, s1_ref, p0_ref, p1_ref, a0_ref, a1_ref):
    qb = pl.program_id(2)
    cols = NSA_HG * QB
    qf = q_ref[...].astype(F32)
    qs_t = jnp.concatenate([qf[:, h * DK:(h + 1) * DK].T for h in range(NSA_HG)], axis=1).astype(BF16)
    g_t = jax.nn.sigmoid(gate_ref[...].astype(F32)).T

    def gate_row(c):
        return jnp.concatenate([g_t[3 * h + c:3 * h + c + 1, :] for h in range(NSA_HG)], axis=1)

    sub = lax.broadcasted_iota(jnp.int32, (128, cols), 0)

    m_ref[...] = jnp.full((1, cols), M_FLOOR, F32)
    l_ref[...] = jnp.zeros((1, cols), F32)
    acc_ref[...] = jnp.zeros((256, cols), F32)

    def cmp_chunk(start, bias):
        kc = kc_ref[0, 0, pl.ds(start, 128), :].astype(BF16)
        vc_t = vc_ref[0, 0, pl.ds(start, 128), :].T
        ms_t = msel_ref[pl.ds(start, 128), :].T
        s = _dot(kc, qs_t) + bias
        m_old = m_ref[...]
        m_new = jnp.maximum(m_old, jnp.max(s, axis=0, keepdims=True))
        alpha = jnp.exp2(m_old - m_new)
        p = jnp.exp2(s - m_new)
        l_ref[...] = alpha * l_ref[...] + jnp.sum(p, axis=0, keepdims=True)
        lhs = jnp.concatenate([vc_t, ms_t], axis=0).astype(BF16)
        acc_ref[...] = alpha * acc_ref[...] + _dot(lhs, p.astype(BF16))
        m_ref[...] = m_new

    near_start = pl.multiple_of(8 * qb, 8)
    cmp_chunk(near_start, tab_ref[0, TAB_CMP:TAB_CMP + 128, :]
              + jnp.where(sub + (8 * qb - CMP_PAD) < 0, NEG, 0.0))

    def far_body(c, carry):
        st = pl.multiple_of(CMP_PAD + 128 * c, 8)
        cmp_chunk(st, tab_ref[0, TAB_CONST:TAB_CONST + 128, :]
                  + jnp.where(128 * c + sub < 8 * qb - CMP_PAD, 0.0, NEG))
        return carry

    lax.fori_loop(0, (8 * qb + 7) // 128, far_body, 0)

    l = l_ref[...]
    inv = jnp.where(l > 0.0, 1.0 / l, 0.0)
    accv = acc_ref[...]
    out_ref[...] = accv[:DK] * (gate_row(0) * inv)
    u = accv[DK:] * inv
    imp_t = u[:, 0:QB] + u[:, QB:2 * QB] + u[:, 2 * QB:3 * QB] + u[:, 3 * QB:4 * QB]

    tiles = [jnp.maximum(qb + d, 0) for d in range(-4, 1)]
    kwin = jnp.concatenate([kw_ref[pl.ds(pl.multiple_of(t * QB, QB), QB), :] for t in tiles], axis=0)
    vwin_t = jnp.concatenate([vwt_ref[0, 0, 0, t] for t in tiles], axis=1)
    lhs = jnp.concatenate([vwin_t, jnp.ones((ONES_ROWS, 5 * QB), BF16)], axis=0)
    bias = jnp.concatenate(
        [tab_ref[0, TAB_WIN + i * QB:TAB_WIN + (i + 1) * QB, :] + jnp.where(qb + d < 0, NEG, 0.0)
         for i, d in enumerate(range(-4, 1))], axis=0)
    s = _dot(kwin, qs_t) + bias
    p = jnp.exp2(s - jnp.max(s, axis=0, keepdims=True))
    r = _dot(lhs, p.astype(BF16))
    out_ref[...] += gate_row(2) * (r[:DK] / r[DK:DK + 1])

    ji = lax.broadcasted_iota(jnp.int32, (128, QB), 0)
    qi = lax.broadcasted_iota(jnp.int32, (128, QB), 1)
    cur = 2 * qb + (qi >= SEL_BLOCK).astype(jnp.int32)
    forced = (ji == 0) | (ji == cur) | (ji == cur - 1)
    vt0 = jnp.where(ji <= cur, jnp.where(forced, FORCE, imp_t), -1.0)
    vt = vt0
    jio = ji.astype(F32)
    sel_t = jnp.zeros((128, QB), F32)
    for _ in range(SEL_TOPN):
        mx = jnp.max(vt, axis=0, keepdims=True)
        idx = jnp.min(jnp.where(vt == mx, jio, 128.0), axis=0, keepdims=True)
        hit = jio == idx
        sel_t = jnp.where(hit, 1.0, sel_t)
        vt = jnp.where(hit, -jnp.inf, vt)
    sel_t = jnp.where(vt0 >= 0.0, sel_t, 0.0)
    sb_t = (sel_t - 1.0) * (-NEG)
    sb_far = jnp.where(ji < 2 * qb - 2, sb_t, NEG)
    aug_far = (jnp.concatenate([sb_far] * NSA_HG, axis=1) + tab_ref[0, TAB_CONST:TAB_CONST + 128, :]).astype(BF16)
    aug_near = jnp.concatenate([sb_t] * NSA_HG, axis=1).astype(BF16)
    q_far = jnp.concatenate([qs_t, aug_far], axis=0)
    q_near = jnp.concatenate([qs_t, aug_near], axis=0)

    m_ref[...] = jnp.full((1, cols), M_FLOOR, F32)
    acc_ref[...] = jnp.zeros((256, cols), F32)
    ones_t = jnp.ones((ONES_ROWS, 256), BF16)
    n_tiles = e_ref.shape[0] // 256
    arows = DK + ONES_ROWS

    def k_aug(t):
        st = pl.multiple_of(t * QB, QB)
        return jnp.concatenate([ksl_ref[pl.ds(st, QB), :], e_ref[pl.ds(st, QB), :]], axis=1)

    def scores(t):
        tt = jnp.clip(t, 0, n_tiles - 1)
        return _dot(jnp.concatenate([k_aug(2 * tt), k_aug(2 * tt + 1)], axis=0), q_far)

    def softmax_stage(s, p_buf, a_buf):
        m_old = m_ref[...]
        m_new = jnp.maximum(m_old, jnp.max(s, axis=0, keepdims=True))
        a_buf[...] = jnp.exp2(m_old - m_new)
        p_buf[...] = jnp.exp2(s - m_new).astype(BF16)
        m_ref[...] = m_new

    def values_stage(v_t, p_buf, a_buf):
        lhs = jnp.concatenate([v_t, ones_t], axis=0)
        acc_ref[:arows] = a_buf[...] * acc_ref[:arows] + _dot(lhs, p_buf[...])

    def v_far(t):
        tt = jnp.clip(t, 0, n_tiles - 1)
        return jnp.concatenate([vslt_ref[0, 0, 0, 2 * tt], vslt_ref[0, 0, 0, 2 * tt + 1]], axis=1)

    n_far = qb // 2
    s0_ref[...] = scores(0)
    p1_ref[...] = jnp.zeros(p1_ref.shape, BF16)
    a1_ref[...] = jnp.ones(a1_ref.shape, F32)

    bufs = ((s0_ref, p0_ref, a0_ref), (s1_ref, p1_ref, a1_ref))

    def sel_far_body(v, carry):
        a = SEL_UNROLL * v
        for k in range(SEL_UNROLL):
            (s_cur, p_cur, a_cur), (s_nxt, p_prev, a_prev) = bufs[k % 2], bufs[1 - k % 2]
            s_nxt[...] = scores(a + k + 1)
            softmax_stage(s_cur[...], p_cur, a_cur)
            values_stage(v_far(a + k - 1), p_prev, a_prev)
        return carry

    n_iter = (n_far + SEL_UNROLL - 1) // SEL_UNROLL
    lax.fori_loop(0, n_iter, sel_far_body, 0)
    values_stage(v_far(SEL_UNROLL * n_iter - 1), p1_ref, a1_ref)

    tp = jnp.maximum(qb - 1, 0)
    first_mask = jnp.where(qb == 0, NEG, 0.0)
    ka = jnp.concatenate([k_aug(tp), k_aug(qb)], axis=0)
    bias = jnp.concatenate([tab_ref[0, TAB_NEAR:TAB_NEAR + QB, :] + first_mask,
                            tab_ref[0, TAB_NEAR + QB:TAB_NEAR + 2 * QB, :]], axis=0)
    softmax_stage(_dot(ka, q_near) + bias, p0_ref, a0_ref)
    values_stage(jnp.concatenate([vslt_ref[0, 0, 0, tp], vslt_ref[0, 0, 0, qb]], axis=1), p0_ref, a0_ref)
    accv = acc_ref[:arows]
    out_ref[...] += gate_row(1) * (accv[:DK] / accv[DK:DK + 1])

    res = out_ref[...]
    for h in range(NSA_HG):
        o_ref[:, h * DK:(h + 1) * DK] = res[:, h * QB:(h + 1) * QB].T.astype(o_ref.dtype)


def _nsa(proj, vt, kvc, msel, e_mat, tab, B, S):
    T = B * S
    NQ = S // QB
    kv0 = COL_KV // DK
    rows_c = kvc.shape[2]
    seq = lambda off: pl.BlockSpec((S, DK), lambda b, g, i: (b, off + g))
    vt_spec = lambda kind: pl.BlockSpec((1, 1, 1, NQ, DK, QB), lambda b, g, i: (kind, b, g, 0, 0, 0))
    return pl.pallas_call(
        _nsa_kernel,
        grid=(B, NSA_G, NQ),
        in_specs=[
            pl.BlockSpec((QB, NSA_HG * DK), lambda b, g, i: (b * NQ + i, g)),
            seq(kv0 + 4), vt_spec(0), seq(kv0 + 8), vt_spec(1),
            pl.BlockSpec((QB, 128), lambda b, g, i: (b * NQ + i, COL_GATE // 128 + g)),
            pl.BlockSpec((1, 1, rows_c, DK), lambda b, g, i: (b, g, 0, 0)),
            pl.BlockSpec((1, 1, rows_c, DK), lambda b, g, i: (b, 2 + g, 0, 0)),
            pl.BlockSpec(msel.shape, lambda b, g, i: (0, 0)),
            pl.BlockSpec(e_mat.shape, lambda b, g, i: (0, 0)),
            pl.BlockSpec((1, TAB_ROWS, NSA_HG * QB), lambda b, g, i: (g, 0, 0)),
        ],
        out_specs=pl.BlockSpec((QB, NSA_HG * DK), lambda b, g, i: (b * NQ + i, g)),
        out_shape=jax.ShapeDtypeStruct((T, NSA_HEADS * DK), BF16),
        scratch_shapes=[
            pltpu.VMEM((256, NSA_HG * QB), F32),
            pltpu.VMEM((1, NSA_HG * QB), F32),
            pltpu.VMEM((1, NSA_HG * QB), F32),
            pltpu.VMEM((DK, NSA_HG * QB), F32),
            pltpu.VMEM((256, NSA_HG * QB), F32), pltpu.VMEM((256, NSA_HG * QB), F32),
            pltpu.VMEM((256, NSA_HG * QB), BF16), pltpu.VMEM((256, NSA_HG * QB), BF16),
            pltpu.VMEM((1, NSA_HG * QB), F32), pltpu.VMEM((1, NSA_HG * QB), F32),
        ],
        compiler_params=pltpu.CompilerParams(
            dimension_semantics=("parallel", "parallel", "arbitrary"), vmem_limit_bytes=VMEM_LIMIT),
        name="nsa_attention",
    )(proj, proj, vt, proj, vt, proj, kvc, kvc, msel, e_mat, tab)


def _retention_kernel(q_ref, k_ref, v_ref, g_ref, cos_ref, sin_ref, dec_ref, xz_ref, gc_ref, o_ref, r_scr):
    @pl.when(pl.program_id(1) == 0)
    def _():
        r_scr[...] = jnp.zeros(r_scr.shape, F32)

    cos = cos_ref[...]
    sin = sin_ref[...]
    hd = RET_D // 2

    def rot(x):
        x1, x2 = x[:, :hd], x[:, hd:]
        return jnp.concatenate([x1 * cos - x2 * sin, x2 * cos + x1 * sin], axis=1)

    for h in range(RET_HEADS):
        sl = slice(h * RET_D, (h + 1) * RET_D)
        qh = rot(q_ref[:, sl].astype(F32))
        kh = rot(k_ref[:, sl].astype(F32))
        vh = v_ref[:, sl]
        xi = xz_ref[h, 0]
        zeta = xz_ref[h, 1]
        qb16 = qh.astype(BF16)
        inner = _dot_nt(qb16, kh.astype(BF16)) * dec_ref[h]
        o = _dot(inner.astype(BF16), vh)
        r_old = r_scr[h]
        cross = _dot(qb16, r_old.astype(BF16))
        o = o + cross * jnp.concatenate([xi, xi], axis=1)
        kz = (kh * jnp.concatenate([zeta, zeta], axis=1)).astype(BF16)
        r_scr[h] = r_old * gc_ref[h] + _dot(kz.T, vh)
        mu = jnp.mean(o, axis=-1, keepdims=True)
        var = jnp.mean(jnp.square(o - mu), axis=-1, keepdims=True)
        on = (o - mu) * lax.rsqrt(var + GN_EPS)
        gf = g_ref[:, sl].astype(F32)
        o_ref[:, sl] = (gf * jax.nn.sigmoid(gf) * on).astype(o_ref.dtype)


def _retention(proj, cos, sin, decay, xz, gc, B, S):
    T = B * S
    N = S // RET_CHUNK
    W = RET_HEADS * RET_D
    sec = lambda c0: pl.BlockSpec((RET_CHUNK, W), lambda b, n: (b * N + n, c0 // W))
    return pl.pallas_call(
        _retention_kernel,
        grid=(B, N),
        in_specs=[
            sec(COL_QR), sec(COL_KR), sec(COL_VR), sec(COL_GR),
            pl.BlockSpec((RET_CHUNK, RET_D // 2), lambda b, n: (n, 0)),
            pl.BlockSpec((RET_CHUNK, RET_D // 2), lambda b, n: (n, 0)),
            pl.BlockSpec((RET_HEADS, RET_CHUNK, RET_CHUNK), lambda b, n: (0, 0, 0)),
            pl.BlockSpec((RET_HEADS, 2, RET_CHUNK, 128), lambda b, n: (0, 0, 0, 0)),
            pl.BlockSpec(memory_space=pltpu.SMEM),
        ],
        out_specs=pl.BlockSpec((RET_CHUNK, W), lambda b, n: (b * N + n, 0)),
        out_shape=jax.ShapeDtypeStruct((T, W), BF16),
        scratch_shapes=[pltpu.VMEM((RET_HEADS, RET_D, RET_D), F32)],
        compiler_params=pltpu.CompilerParams(
            dimension_semantics=("parallel", "arbitrary"), vmem_limit_bytes=VMEM_LIMIT),
        name="retention",
    )(proj, proj, proj, proj, cos, sin, decay, xz, gc)


def _rms(v, g):
    return v * lax.rsqrt(jnp.mean(v * v, axis=-1, keepdims=True) + NORM_EPS) * g


def _outproj_kernel(oa_ref, or_ref, w_ref, x_ref, gpost_ref, gpre_ref, x1_ref, h2_ref):
    wa = oa_ref.shape[1]
    mix = _dot(oa_ref[...], w_ref[:wa, :]) + _dot(or_ref[...], w_ref[wa:, :])
    x1 = x_ref[...] + _rms(mix, gpost_ref[...])
    x1_ref[...] = x1
    h2_ref[...] = _rms(x1, gpre_ref[...]).astype(h2_ref.dtype)


def _out_proj(oa, orr, w_out, x2, g_post, g_pre, tm=512):
    T, D = x2.shape
    tm = min(tm, T)
    row = lambda w: pl.BlockSpec((tm, w), lambda i: (i, 0))
    full = lambda a: pl.BlockSpec(a.shape, lambda i: (0,) * a.ndim)
    return pl.pallas_call(
        _outproj_kernel,
        grid=(T // tm,),
        in_specs=[row(oa.shape[1]), row(orr.shape[1]), full(w_out), row(D), full(g_post), full(g_pre)],
        out_specs=[row(D), row(D)],
        out_shape=[jax.ShapeDtypeStruct((T, D), F32), jax.ShapeDtypeStruct((T, D), BF16)],
        compiler_params=pltpu.CompilerParams(
            dimension_semantics=("parallel",), vmem_limit_bytes=VMEM_LIMIT),
        name="out_proj",
    )(oa, orr, w_out, x2, g_post, g_pre)


def _mlp_kernel(h_ref, wu_ref, wd_ref, x1_ref, g_ref, o_ref):
    f = pl.program_id(1)

    @pl.when(f == 0)
    def _():
        o_ref[...] = jnp.zeros(o_ref.shape, F32)

    u = jnp.maximum(_dot(h_ref[...], wu_ref[...]), 0.0)
    o_ref[...] += _dot((u * u).astype(BF16), wd_ref[...])

    @pl.when(f == pl.num_programs(1) - 1)
    def _():
        o_ref[...] = x1_ref[...] + _rms(o_ref[...], g_ref[...])


def _mlp(h2, w_up, w_down, x1, g_post, tm=1024, tf=512):
    T, D = x1.shape
    F = w_up.shape[1]
    tm = min(tm, T)
    return pl.pallas_call(
        _mlp_kernel,
        grid=(T // tm, F // tf),
        in_specs=[
            pl.BlockSpec((tm, D), lambda i, f: (i, 0)),
            pl.BlockSpec((D, tf), lambda i, f: (0, f)),
            pl.BlockSpec((tf, D), lambda i, f: (f, 0)),
            pl.BlockSpec((tm, D), lambda i, f: (i, 0), pipeline_mode=pl.Buffered(1)),
            pl.BlockSpec((1, D), lambda i, f: (0, 0)),
        ],
        out_specs=pl.BlockSpec((tm, D), lambda i, f: (i, 0)),
        out_shape=jax.ShapeDtypeStruct((T, D), F32),
        compiler_params=pltpu.CompilerParams(
            dimension_semantics=("parallel", "arbitrary"), vmem_limit_bytes=VMEM_LIMIT),
        name="mlp",
    )(h2, w_up, w_down, x1, g_post)


def _t5_bucket_np(rel):
    n = np.maximum(rel, 0)
    max_exact = T5_BUCKETS // 2
    nf = np.maximum(n, 1).astype(np.float64)
    large = max_exact + (np.log(nf / max_exact) / np.log(T5_MAX_DIST / max_exact)
                         * (T5_BUCKETS - max_exact)).astype(np.int64)
    large = np.minimum(large, T5_BUCKETS - 1)
    return np.where(n < max_exact, n, large)


@functools.lru_cache(maxsize=None)
def _static_tables(S):
    masked = T5_BUCKETS
    i = np.arange(QB)[None, :]

    def bucket_rows(rel, valid):
        return np.where(valid, _t5_bucket_np(rel), masked)

    kk = np.arange(256)[:, None]
    rel = i - kk + QB
    b_near = bucket_rows(rel, rel >= 0)
    kk = np.arange(640)[:, None]
    rel = i - kk + WINDOW
    b_win = bucket_rows(rel, (rel >= 0) & (rel < WINDOW))
    m = np.arange(128)[:, None]
    rel = i - CMP_STRIDE * (m - CMP_PAD) - (CMP_BLOCK - 1)
    b_cmp = bucket_rows(rel, rel >= 0)
    b_const = np.full((128, QB), T5_BUCKETS - 1)
    bucket_idx = np.concatenate([b_near, b_win, b_cmp, b_const], axis=0).astype(np.int32)
    n_cmp = (S - CMP_BLOCK) // CMP_STRIDE + 1
    n_sel = S // SEL_BLOCK
    cs = np.arange(n_cmp) * CMP_STRIDE
    ss = np.arange(n_sel) * SEL_BLOCK
    overlap = (cs[:, None] <= ss[None, :] + SEL_BLOCK - 1) & (cs[:, None] + CMP_BLOCK - 1 >= ss[None, :])
    rows = CMP_PAD + S // CMP_STRIDE + 8
    msel = np.zeros((rows, 128), np.float32)
    msel[CMP_PAD:CMP_PAD + n_cmp, :n_sel] = overlap
    e_mat = (np.arange(S)[:, None] // SEL_BLOCK == np.arange(128)[None, :]).astype(np.float32)
    log_gamma = np.log(1.0 - np.exp2(-5.0 - np.arange(RET_HEADS, dtype=np.float32))).astype(np.float32)
    idx = np.arange(RET_CHUNK, dtype=np.float32)
    diff = idx[:, None] - idx[None, :]
    decay = np.where(diff[None] >= 0, np.exp(np.maximum(diff, 0.0)[None] * log_gamma[:, None, None]), 0.0)
    xi = np.exp((idx + 1.0)[None, :] * log_gamma[:, None])
    zeta = np.exp((RET_CHUNK - 1.0 - idx)[None, :] * log_gamma[:, None])
    xz = np.stack([np.broadcast_to(xi[:, :, None], (RET_HEADS, RET_CHUNK, 128)),
                   np.broadcast_to(zeta[:, :, None], (RET_HEADS, RET_CHUNK, 128))], axis=1)
    g_c = np.exp(RET_CHUNK * log_gamma)
    inv_freq = ROPE_BASE ** (-np.arange(0, RET_D, 2, dtype=np.float32) / RET_D)
    ang = np.arange(S, dtype=np.float32)[:, None] * inv_freq[None, :]
    return dict(bucket_idx=bucket_idx, msel=msel, e_mat=e_mat, decay=decay.astype(np.float32),
                xz=xz.astype(np.float32), g_c=g_c.astype(np.float32), ang=ang.astype(np.float32))


def _bias_table(t5_bias, st):
    assert st["bucket_idx"].shape == (TAB_ROWS, QB)
    idx = jnp.asarray(st["bucket_idx"].reshape(-1))
    onehot = (idx[None, :] == jnp.arange(T5_BUCKETS + 1, dtype=jnp.int32)[:, None]).astype(F32)
    vals = jnp.concatenate([t5_bias.astype(F32) * LOG2E, jnp.full((1, NSA_HEADS), NEG, F32)], axis=0)
    tab = jnp.einsum("bh,bn->hn", vals, onehot, precision=lax.Precision.HIGHEST)
    tab = tab.reshape(NSA_G, NSA_HG, TAB_ROWS, QB).transpose(0, 2, 1, 3)
    return tab.reshape(NSA_G, TAB_ROWS, NSA_HG * QB)


def kernel(x, norm_mix_pre, w_in, cmp_pe_k, cmp_w1_k, cmp_b1_k, cmp_w2_k, cmp_pe_v, cmp_w1_v, cmp_b1_v,
           cmp_w2_v, t5_bias, w_out, norm_mix_post, norm_mlp_pre, w_up, w_down, norm_mlp_post):
    B, S, D = x.shape
    T = B * S
    depth = w_in.shape[0]
    st = _static_tables(S)
    tab = _bias_table(t5_bias, st)
    msel = jnp.asarray(st["msel"])
    e_mat = jnp.asarray(st["e_mat"], BF16)
    decay = jnp.asarray(st["decay"])
    xz = jnp.asarray(st["xz"])
    gc = jnp.asarray(st["g_c"])
    ang = jnp.asarray(st["ang"])
    cos, sin = jnp.cos(ang), jnp.sin(ang)

    colscale = np.ones((1, N_PROJ), np.float32)
    colscale[0, COL_QA:COL_QA + NSA_HEADS * DK] = DK ** -0.5 * LOG2E
    colscale[0, COL_KR:COL_KR + RET_HEADS * RET_D] = RET_D ** -0.5
    colscale = jnp.asarray(colscale)
    kv_lo = NSA_HEADS * DK
    gate_lo = kv_lo + 6 * NSA_G * DK
    gate_mid = gate_lo + 3 * NSA_HG
    gate_hi = gate_lo + 3 * NSA_HEADS

    xcur = x.reshape(T, D)
    for l in range(depth):
        w = w_in[l]
        w_re = jnp.concatenate(
            [w[:, :kv_lo], w[:, gate_hi:], w[:, kv_lo:gate_lo],
             w[:, gate_lo:gate_mid], jnp.zeros((D, 128 - 3 * NSA_HG), w.dtype),
             w[:, gate_mid:gate_hi], jnp.zeros((D, N_PROJ - COL_GATE - 128 - 3 * NSA_HG), w.dtype)],
            axis=1).astype(BF16)
        proj = _in_proj(xcur, norm_mix_pre[l][None, :], w_re, colscale)

        xc, vt = _relayout(proj, B, S)
        half = CMP_STRIDE * DK
        pe = jnp.stack([cmp_pe_k[l].reshape(2, half), cmp_pe_v[l].reshape(2, half)])
        pe = jnp.concatenate([pe, jnp.zeros((2, 6, half), F32)], axis=1)
        w1 = jnp.stack([cmp_w1_k[l], cmp_w1_v[l]]).astype(BF16)
        b1 = jnp.stack([cmp_b1_k[l], cmp_b1_v[l]])[:, None, :]
        w2 = jnp.stack([cmp_w2_k[l], cmp_w2_v[l]]).astype(BF16)
        kvc = _compress(xc, pe, w1, b1, w2)

        o_a = _nsa(proj, vt, kvc, msel, e_mat, tab, B, S)
        o_r = _retention(proj, cos, sin, decay, xz, gc, B, S)

        x1, h2 = _out_proj(o_a, o_r, w_out[l].astype(BF16), xcur,
                           norm_mix_post[l][None, :], norm_mlp_pre[l][None, :])
        xcur = _mlp(h2, w_up[l].astype(BF16), w_down[l].astype(BF16), x1, norm_mlp_post[l][None, :])
    return xcur.reshape(B, S, D)
```

```python
import functools

import numpy as np
import jax
import jax.numpy as jnp
from jax import lax
from jax.experimental import pallas as pl
from jax.experimental.pallas import tpu as pltpu

F32 = jnp.float32
BF16 = jnp.bfloat16

D_MODEL = 2048
NSA_HEADS = 8
NSA_G = 2
NSA_HG = 4
DK = 128
CMP_BLOCK = 32
CMP_STRIDE = 16
SEL_BLOCK = 64
SEL_TOPN = 16
WINDOW = 512
QB = 128
RET_HEADS = 4
RET_D = 256
RET_CHUNK = 128
ROPE_BASE = 10000.0
GN_EPS = 1e-6
NORM_EPS = 1e-6
T5_BUCKETS = 32
T5_MAX_DIST = 128
D_FF = 4 * D_MODEL
FORCE = 1e4
NEG = -1e30
M_FLOOR = -1e20
LOG2E = 1.4426950408889634

COL_QA = 0
COL_QR = 1024
COL_KR = 2048
COL_VR = 3072
COL_GR = 4096
COL_KV = 5120
COL_GATE = 6656
N_PROJ = 6912
W_KV = NSA_HEADS * DK
W_GATE = W_KV + 6 * NSA_G * DK
W_RET = W_GATE + 3 * NSA_HEADS
CMP_PAD = 120

VMEM_LIMIT = 56 * 1024 * 1024


def _dot(a, b):
    return jnp.dot(a, b, preferred_element_type=F32)


def _dot_nt(a, b):
    return lax.dot_general(a, b, (((1,), (1,)), ((), ())), preferred_element_type=F32)


def _inproj_kernel(x_ref, g_ref, w_ref, cs_ref, o_ref, h_scr):
    @pl.when(pl.program_id(1) == 0)
    def _():
        xf = x_ref[...]
        ms = jnp.mean(xf * xf, axis=-1, keepdims=True)
        h_scr[...] = (xf * lax.rsqrt(ms + NORM_EPS) * g_ref[...]).astype(BF16)

    acc = _dot(h_scr[...], w_ref[...])
    o_ref[...] = (acc * cs_ref[...]).astype(o_ref.dtype)


def _weight_prep_kernel(w_ref, o_ref):
    rows = w_ref.shape[0]

    def copy(dst, src, width):
        for c in range(0, width, 128):
            o_ref[:, dst + c:dst + c + 128] = w_ref[:, src + c:src + c + 128].astype(BF16)

    copy(COL_QA, 0, W_KV)
    copy(COL_QR, W_RET, 4 * RET_HEADS * RET_D)
    copy(COL_KV, W_KV, W_GATE - W_KV)
    ng = 3 * NSA_HG
    for g in range(NSA_G):
        gate = w_ref[:, W_GATE + g * ng:W_GATE + (g + 1) * ng]
        o_ref[:, COL_GATE + g * 128:COL_GATE + (g + 1) * 128] = jnp.concatenate(
            [gate, jnp.zeros((rows, 128 - ng), F32)], axis=1).astype(BF16)


def _weight_prep(w, tr=256):
    D, n_in = w.shape
    return pl.pallas_call(
        _weight_prep_kernel,
        grid=(D // tr,),
        in_specs=[pl.BlockSpec((tr, n_in), lambda i: (i, 0))],
        out_specs=pl.BlockSpec((tr, N_PROJ), lambda i: (i, 0)),
        out_shape=jax.ShapeDtypeStruct((D, N_PROJ), BF16),
        compiler_params=pltpu.CompilerParams(
            dimension_semantics=("parallel",), vmem_limit_bytes=VMEM_LIMIT),
        name="weight_prep",
    )(w)


def _in_proj(x2, gain, w, colscale, tm=1024, tn=768):
    T, D = x2.shape
    N = w.shape[1]
    tm = min(tm, T)
    return pl.pallas_call(
        _inproj_kernel,
        grid=(T // tm, N // tn),
        in_specs=[
            pl.BlockSpec((tm, D), lambda i, j: (i, 0)),
            pl.BlockSpec((1, D), lambda i, j: (0, 0)),
            pl.BlockSpec((D, tn), lambda i, j: (0, j)),
            pl.BlockSpec((1, tn), lambda i, j: (0, j)),
        ],
        out_specs=pl.BlockSpec((tm, tn), lambda i, j: (i, j)),
        out_shape=jax.ShapeDtypeStruct((T, N), BF16),
        scratch_shapes=[pltpu.VMEM((tm, D), BF16)],
        compiler_params=pltpu.CompilerParams(
            dimension_semantics=("parallel", "arbitrary"), vmem_limit_bytes=VMEM_LIMIT),
        name="in_proj",
    )(x2, gain, w, colscale)


def _relayout_kernel(cmp_ref, vs_ref, vw_ref, xc_ref, vt_ref, t_scr):
    tm = cmp_ref.shape[0]
    for s in range(4):
        t_scr[s] = cmp_ref[:, s * DK:(s + 1) * DK].astype(F32)
        for t in range(CMP_STRIDE):
            xc_ref[0, s, :, t * DK:(t + 1) * DK] = t_scr[
                s, pl.ds(t, tm // CMP_STRIDE, stride=CMP_STRIDE), :].astype(BF16)
    for kind, ref in enumerate((vs_ref, vw_ref)):
        v = ref[...].astype(F32)
        for g in range(NSA_G):
            for r in range(tm // QB):
                vt_ref[kind, 0, g, r] = v[r * QB:(r + 1) * QB, g * DK:(g + 1) * DK].T.astype(BF16)


def _relayout(proj, B, S, tm=1024):
    tm = min(tm, S)
    rpb = S // tm
    NC, NQ = S // CMP_STRIDE, S // QB
    return pl.pallas_call(
        _relayout_kernel,
        grid=(B * rpb,),
        in_specs=[
            pl.BlockSpec((tm, 4 * DK), lambda i: (i, COL_KV // (4 * DK))),
            pl.BlockSpec((tm, 2 * DK), lambda i: (i, (COL_KV + 6 * DK) // (2 * DK))),
            pl.BlockSpec((tm, 2 * DK), lambda i: (i, (COL_KV + 10 * DK) // (2 * DK))),
        ],
        out_specs=[
            pl.BlockSpec((1, 4, tm // CMP_STRIDE, CMP_STRIDE * DK), lambda i: (i // rpb, 0, i % rpb, 0)),
            pl.BlockSpec((2, 1, NSA_G, tm // QB, DK, QB), lambda i: (0, i // rpb, 0, i % rpb, 0, 0)),
        ],
        out_shape=[jax.ShapeDtypeStruct((B, 4, NC, CMP_STRIDE * DK), BF16),
                   jax.ShapeDtypeStruct((2, B, NSA_G, NQ, DK, QB), BF16)],
        scratch_shapes=[pltpu.VMEM((4, tm, DK), F32)],
        compiler_params=pltpu.CompilerParams(
            dimension_semantics=("parallel",), vmem_limit_bytes=VMEM_LIMIT),
        name="kv_relayout",
    )(proj, proj, proj)


def _compress_kernel(x_ref, pe_ref, w1_ref, b1_ref, w2_ref, o_ref):
    nc = x_ref.shape[2]
    half = x_ref.shape[3]
    xf = x_ref[0, 0].astype(F32)
    xa = (xf + pe_ref[0, 0:1, :]).astype(BF16)
    xb = (xf + pe_ref[0, 1:2, :]).astype(BF16)
    a = _dot(xa, w1_ref[0, :half, :])
    b = _dot(xb, w1_ref[0, half:, :])
    b_next = pltpu.roll(b, nc - 1, 0)
    hid = jax.nn.gelu(a + b_next + b1_ref[0])
    out = _dot(hid.astype(BF16), w2_ref[0])
    o_ref[0, 0, :CMP_PAD, :] = jnp.zeros((CMP_PAD, DK), F32)
    o_ref[0, 0, CMP_PAD:CMP_PAD + nc, :] = out
    o_ref[0, 0, CMP_PAD + nc:, :] = jnp.zeros((o_ref.shape[2] - CMP_PAD - nc, DK), F32)


def _compress(xc, pe, w1, b1, w2):
    B, _, NC, half = xc.shape
    rows = CMP_PAD + NC + 8
    return pl.pallas_call(
        _compress_kernel,
        grid=(B, 4),
        in_specs=[
            pl.BlockSpec((1, 1, NC, half), lambda b, s: (b, s, 0, 0)),
            pl.BlockSpec((1, 8, half), lambda b, s: (s // 2, 0, 0)),
            pl.BlockSpec((1, 2 * half, DK), lambda b, s: (s // 2, 0, 0)),
            pl.BlockSpec((1, 1, DK), lambda b, s: (s // 2, 0, 0)),
            pl.BlockSpec((1, DK, DK), lambda b, s: (s // 2, 0, 0)),
        ],
        out_specs=pl.BlockSpec((1, 1, rows, DK), lambda b, s: (b, s, 0, 0)),
        out_shape=jax.ShapeDtypeStruct((B, 4, rows, DK), F32),
        compiler_params=pltpu.CompilerParams(
            dimension_semantics=("parallel", "parallel"), vmem_limit_bytes=VMEM_LIMIT),
        name="compress_kv",
    )(xc, pe, w1, b1, w2)


TAB_NEAR, TAB_WIN, TAB_CMP, TAB_CONST, TAB_ROWS = 0, 256, 896, 1024, 1152
ONES_ROWS = 16
SEL_UNROLL = 4


def _nsa_kernel(q_ref, ksl_ref, vslt_ref, kw_ref, vwt_ref, gate_ref, kc_ref, vc_ref,
                msel_ref, e_ref, tab_ref, o_ref, acc_ref, m_ref, l_ref, out_ref, s_ref, p_ref, a_ref):
    qb = pl.program_id(1)
    groups = range(NSA_G)
    cols = NSA_HG * QB
    sub = lax.broadcasted_iota(jnp.int32, (128, cols), 0)
    qs_t, g_t = [], []
    for g in groups:
        qf = q_ref[:, g * cols:(g + 1) * cols].astype(F32)
        qs_t.append(jnp.concatenate([qf[:, h * DK:(h + 1) * DK].T for h in range(NSA_HG)],
                                    axis=1).astype(BF16))
        g_t.append(jax.nn.sigmoid(gate_ref[:, g * 128:(g + 1) * 128].astype(F32)).T)

    def gate_row(g, c):
        return jnp.concatenate([g_t[g][3 * h + c:3 * h + c + 1, :] for h in range(NSA_HG)], axis=1)

    for g in groups:
        m_ref[g] = jnp.full((1, cols), M_FLOOR, F32)
        l_ref[g] = jnp.zeros((1, cols), F32)
        acc_ref[g] = jnp.zeros((256, cols), F32)

    def cmp_chunk(g, start, bias):
        kc = kc_ref[0, g, pl.ds(start, 128), :].astype(BF16)
        vc_t = vc_ref[0, g, pl.ds(start, 128), :].T
        ms_t = msel_ref[pl.ds(start, 128), :].T
        s = _dot(kc, qs_t[g]) + bias
        m_old = m_ref[g]
        m_new = jnp.maximum(m_old, jnp.max(s, axis=0, keepdims=True))
        alpha = jnp.exp2(m_old - m_new)
        p = jnp.exp2(s - m_new)
        l_ref[g] = alpha * l_ref[g] + jnp.sum(p, axis=0, keepdims=True)
        lhs = jnp.concatenate([vc_t, ms_t], axis=0).astype(BF16)
        acc_ref[g] = alpha * acc_ref[g] + _dot(lhs, p.astype(BF16))
        m_ref[g] = m_new

    near_start = pl.multiple_of(8 * qb, 8)
    near_mask = jnp.where(sub + (8 * qb - CMP_PAD) < 0, NEG, 0.0)
    for g in groups:
        cmp_chunk(g, near_start, tab_ref[g, TAB_CMP:TAB_CMP + 128, :] + near_mask)

    def far_body(c, carry):
        st = pl.multiple_of(CMP_PAD + 128 * c, 8)
        far_mask = jnp.where(128 * c + sub < 8 * qb - CMP_PAD, 0.0, NEG)
        for g in groups:
            cmp_chunk(g, st, tab_ref[g, TAB_CONST:TAB_CONST + 128, :] + far_mask)
        return carry

    lax.fori_loop(0, (8 * qb + 7) // 128, far_body, 0)

    imp_t = []
    for g in groups:
        l = l_ref[g]
        inv = jnp.where(l > 0.0, 1.0 / l, 0.0)
        accv = acc_ref[g]
        out_ref[g] = accv[:DK] * (gate_row(g, 0) * inv)
        u = accv[DK:] * inv
        imp_t.append(u[:, 0:QB] + u[:, QB:2 * QB] + u[:, 2 * QB:3 * QB] + u[:, 3 * QB:4 * QB])

    tiles = [jnp.maximum(qb + d, 0) for d in range(-4, 1)]
    win_ones = jnp.ones((ONES_ROWS, 5 * QB), BF16)
    for g in groups:
        kwin = jnp.concatenate([kw_ref[pl.ds(pl.multiple_of(t * QB, QB), QB), g * DK:(g + 1) * DK]
                                for t in tiles], axis=0)
        vwin_t = jnp.concatenate([vwt_ref[0, 0, g, t] for t in tiles], axis=1)
        lhs = jnp.concatenate([vwin_t, win_ones], axis=0)
        bias = jnp.concatenate(
            [tab_ref[g, TAB_WIN + i * QB:TAB_WIN + (i + 1) * QB, :] + jnp.where(qb + d < 0, NEG, 0.0)
             for i, d in enumerate(range(-4, 1))], axis=0)
        s = _dot(kwin, qs_t[g]) + bias
        p = jnp.exp2(s - jnp.max(s, axis=0, keepdims=True))
        r = _dot(lhs, p.astype(BF16))
        out_ref[g] += gate_row(g, 2) * (r[:DK] / r[DK:DK + 1])

    ji = lax.broadcasted_iota(jnp.int32, (128, NSA_G * QB), 0)
    qi = lax.broadcasted_iota(jnp.int32, (128, NSA_G * QB), 1) & (QB - 1)
    cur = 2 * qb + (qi >= SEL_BLOCK).astype(jnp.int32)
    valid = ji <= cur
    forced = (ji == 0) | (ji == cur) | (ji == cur - 1)
    taken = -jnp.inf
    vt = jnp.where(forced, taken, jnp.where(valid, jnp.concatenate(imp_t, axis=1), -1.0))
    jio = ji.astype(F32)
    for _ in range(SEL_TOPN - 3):
        mx = jnp.max(vt, axis=0, keepdims=True)
        idx = jnp.min(jnp.where(vt == mx, jio, 128.0), axis=0, keepdims=True)
        vt = jnp.where(jio == idx, taken, vt)
    sb_all = jnp.where((vt == taken) & valid, 0.0, NEG)
    sb_far_all = jnp.where(ji < 2 * qb - 2, sb_all, NEG)
    q_far, q_near = [], []
    for g in groups:
        sb_t = sb_all[:, g * QB:(g + 1) * QB]
        sb_far = sb_far_all[:, g * QB:(g + 1) * QB]
        aug_far = (jnp.concatenate([sb_far] * NSA_HG, axis=1) + tab_ref[g, TAB_CONST:TAB_CONST + 128, :]).astype(BF16)
        aug_near = jnp.concatenate([sb_t] * NSA_HG, axis=1).astype(BF16)
        q_far.append(jnp.concatenate([qs_t[g], aug_far], axis=0))
        q_near.append(jnp.concatenate([qs_t[g], aug_near], axis=0))

    ones_t = jnp.ones((ONES_ROWS, 256), BF16)
    n_tiles = e_ref.shape[0] // 256
    arows = DK + ONES_ROWS

    def k_aug(g, t):
        st = pl.multiple_of(t * QB, QB)
        return jnp.concatenate([ksl_ref[pl.ds(st, QB), g * DK:(g + 1) * DK], e_ref[pl.ds(st, QB), :]], axis=1)

    def scores(g, t):
        tt = jnp.clip(t, 0, n_tiles - 1)
        return _dot(jnp.concatenate([k_aug(g, 2 * tt), k_aug(g, 2 * tt + 1)], axis=0), q_far[g])

    def softmax_stage(g, s, slot):
        m_old = m_ref[g]
        m_new = jnp.maximum(m_old, jnp.max(s, axis=0, keepdims=True))
        a_ref[g, slot] = jnp.exp2(m_old - m_new)
        p_ref[g, slot] = jnp.exp2(s - m_new).astype(BF16)
        m_ref[g] = m_new

    def values_stage(g, v_t, slot):
        lhs = jnp.concatenate([v_t, ones_t], axis=0)
        acc_ref[g, :arows] = a_ref[g, slot] * acc_ref[g, :arows] + _dot(lhs, p_ref[g, slot])

    def v_far(g, t):
        tt = jnp.clip(t, 0, n_tiles - 1)
        return jnp.concatenate([vslt_ref[0, 0, g, 2 * tt], vslt_ref[0, 0, g, 2 * tt + 1]], axis=1)

    n_far = qb // 2
    for g in groups:
        m_ref[g] = jnp.full((1, cols), M_FLOOR, F32)
        acc_ref[g] = jnp.zeros((256, cols), F32)
        s_ref[g, 0] = scores(g, 0)
        p_ref[g, 1] = jnp.zeros((256, cols), BF16)
        a_ref[g, 1] = jnp.ones((1, cols), F32)

    def sel_far_body(v, carry):
        a = SEL_UNROLL * v
        for k in range(SEL_UNROLL):
            cur_slot, other = k % 2, 1 - k % 2
            for g in groups:
                s_ref[g, other] = scores(g, a + k + 1)
                softmax_stage(g, s_ref[g, cur_slot], cur_slot)
                values_stage(g, v_far(g, a + k - 1), other)
        return carry

    n_iter = (n_far + SEL_UNROLL - 1) // SEL_UNROLL
    lax.fori_loop(0, n_iter, sel_far_body, 0)

    tp = jnp.maximum(qb - 1, 0)
    first_mask = jnp.where(qb == 0, NEG, 0.0)
    for g in groups:
        values_stage(g, v_far(g, SEL_UNROLL * n_iter - 1), 1)
        ka = jnp.concatenate([k_aug(g, tp), k_aug(g, qb)], axis=0)
        bias = jnp.concatenate([tab_ref[g, TAB_NEAR:TAB_NEAR + QB, :] + first_mask,
                                tab_ref[g, TAB_NEAR + QB:TAB_NEAR + 2 * QB, :]], axis=0)
        softmax_stage(g, _dot(ka, q_near[g]) + bias, 0)
        values_stage(g, jnp.concatenate([vslt_ref[0, 0, g, tp], vslt_ref[0, 0, g, qb]], axis=1), 0)
        accv = acc_ref[g, :arows]
        res = out_ref[g] + gate_row(g, 1) * (accv[:DK] / accv[DK:DK + 1])
        for h in range(NSA_HG):
            c0 = (g * NSA_HG + h) * DK
            o_ref[:, c0:c0 + DK] = res[:, h * QB:(h + 1) * QB].T.astype(o_ref.dtype)


def _nsa(proj, vt, kvc, msel, e_mat, tab, B, S):
    T = B * S
    NQ = S // QB
    cols = NSA_HG * QB
    rows_c = kvc.shape[2]
    once = pl.Buffered(1)
    seq = lambda c0: pl.BlockSpec((S, NSA_G * DK), lambda b, i: (b, c0 // (NSA_G * DK)), pipeline_mode=once)
    vt_spec = lambda kind: pl.BlockSpec((1, 1, NSA_G, NQ, DK, QB), lambda b, i: (kind, b, 0, 0, 0, 0),
                                        pipeline_mode=once)
    const = lambda a: pl.BlockSpec(a.shape, lambda b, i: (0,) * a.ndim, pipeline_mode=once)
    return pl.pallas_call(
        _nsa_kernel,
        grid=(B, NQ),
        in_specs=[
            pl.BlockSpec((QB, NSA_HEADS * DK), lambda b, i: (b * NQ + i, 0)),
            seq(COL_KV + 4 * DK), vt_spec(0), seq(COL_KV + 8 * DK), vt_spec(1),
            pl.BlockSpec((QB, NSA_G * 128), lambda b, i: (b * NQ + i, COL_GATE // (NSA_G * 128))),
            pl.BlockSpec((1, NSA_G, rows_c, DK), lambda b, i: (b, 0, 0, 0)),
            pl.BlockSpec((1, NSA_G, rows_c, DK), lambda b, i: (b, 1, 0, 0)),
            const(msel), const(e_mat), const(tab),
        ],
        out_specs=pl.BlockSpec((QB, NSA_HEADS * DK), lambda b, i: (b * NQ + i, 0)),
        out_shape=jax.ShapeDtypeStruct((T, NSA_HEADS * DK), BF16),
        scratch_shapes=[
            pltpu.VMEM((NSA_G, 256, cols), F32),
            pltpu.VMEM((NSA_G, 1, cols), F32),
            pltpu.VMEM((NSA_G, 1, cols), F32),
            pltpu.VMEM((NSA_G, DK, cols), F32),
            pltpu.VMEM((NSA_G, 2, 256, cols), F32),
            pltpu.VMEM((NSA_G, 2, 256, cols), BF16),
            pltpu.VMEM((NSA_G, 2, 1, cols), F32),
        ],
        compiler_params=pltpu.CompilerParams(
            dimension_semantics=("parallel", "arbitrary"), vmem_limit_bytes=VMEM_LIMIT),
        name="nsa_attention",
    )(proj, proj, vt, proj, vt, proj, kvc, kvc, msel, e_mat, tab)


def _retention_kernel(q_ref, k_ref, v_ref, g_ref, cos_ref, sin_ref, dec_ref, xz_ref, gc_ref, o_ref, r_scr):
    @pl.when(pl.program_id(1) == 0)
    def _():
        r_scr[...] = jnp.zeros(r_scr.shape, F32)

    cos = cos_ref[...]
    sin = sin_ref[...]
    hd = RET_D // 2

    def rot(x):
        x1, x2 = x[:, :hd], x[:, hd:]
        return jnp.concatenate([x1 * cos - x2 * sin, x2 * cos + x1 * sin], axis=1)

    for h in range(RET_HEADS):
        sl = slice(h * RET_D, (h + 1) * RET_D)
        qh = rot(q_ref[:, sl].astype(F32))
        kh = rot(k_ref[:, sl].astype(F32))
        vh = v_ref[:, sl]
        xi = xz_ref[h, 0]
        zeta = xz_ref[h, 1]
        qb16 = qh.astype(BF16)
        inner = _dot_nt(qb16, kh.astype(BF16)) * dec_ref[h]
        o = _dot(inner.astype(BF16), vh)
        r_old = r_scr[h]
        cross = _dot(qb16, r_old.astype(BF16))
        o = o + cross * jnp.concatenate([xi, xi], axis=1)
        kz = (kh * jnp.concatenate([zeta, zeta], axis=1)).astype(BF16)
        r_scr[h] = r_old * gc_ref[h] + _dot(kz.T, vh)
        mu = jnp.mean(o, axis=-1, keepdims=True)
        var = jnp.mean(jnp.square(o - mu), axis=-1, keepdims=True)
        on = (o - mu) * lax.rsqrt(var + GN_EPS)
        gf = g_ref[:, sl].astype(F32)
        o_ref[:, sl] = (gf * jax.nn.sigmoid(gf) * on).astype(o_ref.dtype)


def _retention(proj, cos, sin, decay, xz, gc, B, S):
    T = B * S
    N = S // RET_CHUNK
    W = RET_HEADS * RET_D
    sec = lambda c0: pl.BlockSpec((RET_CHUNK, W), lambda b, n: (b * N + n, c0 // W))
    return pl.pallas_call(
        _retention_kernel,
        grid=(B, N),
        in_specs=[
            sec(COL_QR), sec(COL_KR), sec(COL_VR), sec(COL_GR),
            pl.BlockSpec((RET_CHUNK, RET_D // 2), lambda b, n: (n, 0)),
            pl.BlockSpec((RET_CHUNK, RET_D // 2), lambda b, n: (n, 0)),
            pl.BlockSpec((RET_HEADS, RET_CHUNK, RET_CHUNK), lambda b, n: (0, 0, 0)),
            pl.BlockSpec((RET_HEADS, 2, RET_CHUNK, 128), lambda b, n: (0, 0, 0, 0)),
            pl.BlockSpec(memory_space=pltpu.SMEM),
        ],
        out_specs=pl.BlockSpec((RET_CHUNK, W), lambda b, n: (b * N + n, 0)),
        out_shape=jax.ShapeDtypeStruct((T, W), BF16),
        scratch_shapes=[pltpu.VMEM((RET_HEADS, RET_D, RET_D), F32)],
        compiler_params=pltpu.CompilerParams(
            dimension_semantics=("parallel", "arbitrary"), vmem_limit_bytes=VMEM_LIMIT),
        name="retention",
    )(proj, proj, proj, proj, cos, sin, decay, xz, gc)


def _rms(v, g):
    return v * lax.rsqrt(jnp.mean(v * v, axis=-1, keepdims=True) + NORM_EPS) * g


def _outproj_kernel(oa_ref, or_ref, w_ref, x_ref, gpost_ref, gpre_ref, x1_ref, h2_ref):
    wa = oa_ref.shape[1]
    mix = _dot(oa_ref[...], w_ref[:wa, :]) + _dot(or_ref[...], w_ref[wa:, :])
    x1 = x_ref[...] + _rms(mix, gpost_ref[...])
    x1_ref[...] = x1
    h2_ref[...] = _rms(x1, gpre_ref[...]).astype(h2_ref.dtype)


def _out_proj(oa, orr, w_out, x2, g_post, g_pre, tm=512):
    T, D = x2.shape
    tm = min(tm, T)
    row = lambda w: pl.BlockSpec((tm, w), lambda i: (i, 0))
    full = lambda a: pl.BlockSpec(a.shape, lambda i: (0,) * a.ndim)
    return pl.pallas_call(
        _outproj_kernel,
        grid=(T // tm,),
        in_specs=[row(oa.shape[1]), row(orr.shape[1]), full(w_out), row(D), full(g_post), full(g_pre)],
        out_specs=[row(D), row(D)],
        out_shape=[jax.ShapeDtypeStruct((T, D), F32), jax.ShapeDtypeStruct((T, D), BF16)],
        compiler_params=pltpu.CompilerParams(
            dimension_semantics=("parallel",), vmem_limit_bytes=VMEM_LIMIT),
        name="out_proj",
    )(oa, orr, w_out, x2, g_post, g_pre)


def _mlp_kernel(h_ref, wu_ref, wd_ref, x1_ref, g_ref, o_ref):
    f = pl.program_id(1)

    @pl.when(f == 0)
    def _():
        o_ref[...] = jnp.zeros(o_ref.shape, F32)

    u = jnp.maximum(_dot(h_ref[...], wu_ref[...]), 0.0)
    o_ref[...] += _dot((u * u).astype(BF16), wd_ref[...])

    @pl.when(f == pl.num_programs(1) - 1)
    def _():
        o_ref[...] = x1_ref[...] + _rms(o_ref[...], g_ref[...])


def _mlp(h2, w_up, w_down, x1, g_post, tm=1024, tf=512):
    T, D = x1.shape
    F = w_up.shape[1]
    tm = min(tm, T)
    return pl.pallas_call(
        _mlp_kernel,
        grid=(T // tm, F // tf),
        in_specs=[
            pl.BlockSpec((tm, D), lambda i, f: (i, 0)),
            pl.BlockSpec((D, tf), lambda i, f: (0, f)),
            pl.BlockSpec((tf, D), lambda i, f: (f, 0)),
            pl.BlockSpec((tm, D), lambda i, f: (i, 0), pipeline_mode=pl.Buffered(1)),
            pl.BlockSpec((1, D), lambda i, f: (0, 0)),
        ],
        out_specs=pl.BlockSpec((tm, D), lambda i, f: (i, 0)),
        out_shape=jax.ShapeDtypeStruct((T, D), F32),
        compiler_params=pltpu.CompilerParams(
            dimension_semantics=("parallel", "arbitrary"), vmem_limit_bytes=VMEM_LIMIT),
        name="mlp",
    )(h2, w_up, w_down, x1, g_post)


def _t5_bucket_np(rel):
    n = np.maximum(rel, 0)
    max_exact = T5_BUCKETS // 2
    nf = np.maximum(n, 1).astype(np.float64)
    large = max_exact + (np.log(nf / max_exact) / np.log(T5_MAX_DIST / max_exact)
                         * (T5_BUCKETS - max_exact)).astype(np.int64)
    large = np.minimum(large, T5_BUCKETS - 1)
    return np.where(n < max_exact, n, large)


@functools.lru_cache(maxsize=None)
def _static_tables(S):
    masked = T5_BUCKETS
    i = np.arange(QB)[None, :]

    def bucket_rows(rel, valid):
        return np.where(valid, _t5_bucket_np(rel), masked)

    kk = np.arange(256)[:, None]
    rel = i - kk + QB
    b_near = bucket_rows(rel, rel >= 0)
    kk = np.arange(640)[:, None]
    rel = i - kk + WINDOW
    b_win = bucket_rows(rel, (rel >= 0) & (rel < WINDOW))
    m = np.arange(128)[:, None]
    rel = i - CMP_STRIDE * (m - CMP_PAD) - (CMP_BLOCK - 1)
    b_cmp = bucket_rows(rel, rel >= 0)
    b_const = np.full((128, QB), T5_BUCKETS - 1)
    bucket_idx = np.concatenate([b_near, b_win, b_cmp, b_const], axis=0).astype(np.int32)
    n_cmp = (S - CMP_BLOCK) // CMP_STRIDE + 1
    n_sel = S // SEL_BLOCK
    cs = np.arange(n_cmp) * CMP_STRIDE
    ss = np.arange(n_sel) * SEL_BLOCK
    overlap = (cs[:, None] <= ss[None, :] + SEL_BLOCK - 1) & (cs[:, None] + CMP_BLOCK - 1 >= ss[None, :])
    rows = CMP_PAD + S // CMP_STRIDE + 8
    msel = np.zeros((rows, 128), np.float32)
    msel[CMP_PAD:CMP_PAD + n_cmp, :n_sel] = overlap
    e_mat = (np.arange(S)[:, None] // SEL_BLOCK == np.arange(128)[None, :]).astype(np.float32)
    log_gamma = np.log(1.0 - np.exp2(-5.0 - np.arange(RET_HEADS, dtype=np.float32))).astype(np.float32)
    idx = np.arange(RET_CHUNK, dtype=np.float32)
    diff = idx[:, None] - idx[None, :]
    decay = np.where(diff[None] >= 0, np.exp(np.maximum(diff, 0.0)[None] * log_gamma[:, None, None]), 0.0)
    xi = np.exp((idx + 1.0)[None, :] * log_gamma[:, None])
    zeta = np.exp((RET_CHUNK - 1.0 - idx)[None, :] * log_gamma[:, None])
    xz = np.stack([np.broadcast_to(xi[:, :, None], (RET_HEADS, RET_CHUNK, 128)),
                   np.broadcast_to(zeta[:, :, None], (RET_HEADS, RET_CHUNK, 128))], axis=1)
    g_c = np.exp(RET_CHUNK * log_gamma)
    inv_freq = ROPE_BASE ** (-np.arange(0, RET_D, 2, dtype=np.float32) / RET_D)
    ang = np.arange(S, dtype=np.float32)[:, None] * inv_freq[None, :]
    return dict(bucket_idx=bucket_idx, msel=msel, e_mat=e_mat, decay=decay.astype(np.float32),
                xz=xz.astype(np.float32), g_c=g_c.astype(np.float32), ang=ang.astype(np.float32))


def _bias_table(t5_bias, st):
    assert st["bucket_idx"].shape == (TAB_ROWS, QB)
    idx = jnp.asarray(st["bucket_idx"].reshape(-1))
    onehot = (idx[None, :] == jnp.arange(T5_BUCKETS + 1, dtype=jnp.int32)[:, None]).astype(F32)
    vals = jnp.concatenate([t5_bias.astype(F32) * LOG2E, jnp.full((1, NSA_HEADS), NEG, F32)], axis=0)
    tab = jnp.einsum("bh,bn->hn", vals, onehot, precision=lax.Precision.HIGHEST)
    tab = tab.reshape(NSA_G, NSA_HG, TAB_ROWS, QB).transpose(0, 2, 1, 3)
    return tab.reshape(NSA_G, TAB_ROWS, NSA_HG * QB)


def kernel(x, norm_mix_pre, w_in, cmp_pe_k, cmp_w1_k, cmp_b1_k, cmp_w2_k, cmp_pe_v, cmp_w1_v, cmp_b1_v,
           cmp_w2_v, t5_bias, w_out, norm_mix_post, norm_mlp_pre, w_up, w_down, norm_mlp_post):
    B, S, D = x.shape
    T = B * S
    depth = w_in.shape[0]
    st = _static_tables(S)
    tab = _bias_table(t5_bias, st)
    msel = jnp.asarray(st["msel"])
    e_mat = jnp.asarray(st["e_mat"], BF16)
    decay = jnp.asarray(st["decay"])
    xz = jnp.asarray(st["xz"])
    gc = jnp.asarray(st["g_c"])
    ang = jnp.asarray(st["ang"])
    cos, sin = jnp.cos(ang), jnp.sin(ang)

    colscale = np.ones((1, N_PROJ), np.float32)
    colscale[0, COL_QA:COL_QA + NSA_HEADS * DK] = DK ** -0.5 * LOG2E
    colscale[0, COL_KR:COL_KR + RET_HEADS * RET_D] = RET_D ** -0.5
    colscale = jnp.asarray(colscale)

    xcur = x.reshape(T, D)
    for l in range(depth):
        proj = _in_proj(xcur, norm_mix_pre[l][None, :], _weight_prep(w_in[l]), colscale)

        xc, vt = _relayout(proj, B, S)
        half = CMP_STRIDE * DK
        pe = jnp.stack([cmp_pe_k[l].reshape(2, half), cmp_pe_v[l].reshape(2, half)])
        pe = jnp.concatenate([pe, jnp.zeros((2, 6, half), F32)], axis=1)
        w1 = jnp.stack([cmp_w1_k[l], cmp_w1_v[l]]).astype(BF16)
        b1 = jnp.stack([cmp_b1_k[l], cmp_b1_v[l]])[:, None, :]
        w2 = jnp.stack([cmp_w2_k[l], cmp_w2_v[l]]).astype(BF16)
        kvc = _compress(xc, pe, w1, b1, w2)

        o_a = _nsa(proj, vt, kvc, msel, e_mat, tab, B, S)
        o_r = _retention(proj, cos, sin, decay, xz, gc, B, S)

        x1, h2 = _out_proj(o_a, o_r, w_out[l].astype(BF16), xcur,
                           norm_mix_post[l][None, :], norm_mlp_pre[l][None, :])
        xcur = _mlp(h2, w_up[l].astype(BF16), w_down[l].astype(BF16), x1, norm_mlp_post[l][None, :])
    return xcur.reshape(B, S, D)
```

```python
import functools

import numpy as np
import jax
import jax.numpy as jnp
from jax import lax
from jax.experimental import pallas as pl
from jax.experimental.pallas import tpu as pltpu

F32 = jnp.float32
BF16 = jnp.bfloat16

D_MODEL = 2048
NSA_HEADS = 8
NSA_G = 2
NSA_HG = 4
DK = 128
CMP_BLOCK = 32
CMP_STRIDE = 16
SEL_BLOCK = 64
SEL_TOPN = 16
WINDOW = 512
QB = 128
RET_HEADS = 4
RET_D = 256
RET_CHUNK = 128
ROPE_BASE = 10000.0
GN_EPS = 1e-6
NORM_EPS = 1e-6
T5_BUCKETS = 32
T5_MAX_DIST = 128
D_FF = 4 * D_MODEL
FORCE = 1e4
NEG = -1e30
M_FLOOR = -1e20
LOG2E = 1.4426950408889634

COL_QA = 0
COL_QR = 1024
COL_KR = 2048
COL_VR = 3072
COL_GR = 4096
COL_KV = 5120
COL_GATE = 6656
N_PROJ = 6912
W_KV = NSA_HEADS * DK
W_GATE = W_KV + 6 * NSA_G * DK
W_RET = W_GATE + 3 * NSA_HEADS
CMP_PAD = 120

VMEM_LIMIT = 56 * 1024 * 1024


def _dot(a, b):
    return jnp.dot(a, b, preferred_element_type=F32)


def _dot_nt(a, b):
    return lax.dot_general(a, b, (((1,), (1,)), ((), ())), preferred_element_type=F32)


def _inproj_kernel(x_ref, g_ref, w_ref, cs_ref, o_ref, h_scr):
    @pl.when(pl.program_id(1) == 0)
    def _():
        xf = x_ref[...]
        ms = jnp.mean(xf * xf, axis=-1, keepdims=True)
        h_scr[...] = (xf * lax.rsqrt(ms + NORM_EPS) * g_ref[...]).astype(BF16)

    acc = _dot_nt(h_scr[...], w_ref[...])
    o_ref[...] = (acc * cs_ref[...]).astype(o_ref.dtype)


PREP_ROWS = 256
PREP_T_RET = COL_QR // PREP_ROWS
PREP_T_KV = COL_KV // PREP_ROWS
PREP_T_GATE = COL_GATE // PREP_ROWS
PREP_SHIFT = W_RET % PREP_ROWS


def _weight_prep_kernel(a_ref, b_ref, o_ref):
    i = pl.program_id(0)
    aligned = (i < PREP_T_RET) | ((i >= PREP_T_KV) & (i < PREP_T_GATE))

    @pl.when(aligned)
    def _():
        o_ref[...] = a_ref[...].astype(BF16)

    @pl.when((i >= PREP_T_RET) & (i < PREP_T_KV))
    def _():
        o_ref[...] = jnp.concatenate([a_ref[PREP_SHIFT:, :], b_ref[:PREP_SHIFT, :]], axis=0).astype(BF16)

    @pl.when(i == PREP_T_GATE)
    def _():
        ng = 3 * NSA_HG
        a = a_ref[:128, :]
        row = lax.broadcasted_iota(jnp.int32, a.shape, 0)
        g0 = jnp.where(row < ng, a, 0.0)
        g1 = jnp.where(row < ng, pltpu.roll(a, 128 - ng, 0), 0.0)
        o_ref[...] = jnp.concatenate([g0, g1], axis=0).astype(BF16)


def _weight_prep(w_t):
    n_in, D = w_t.shape
    assert W_KV % PREP_ROWS == 0 and W_GATE % PREP_ROWS == 0 and N_PROJ - COL_GATE == PREP_ROWS
    src_ret = W_RET // PREP_ROWS - PREP_T_RET
    src_kv = W_KV // PREP_ROWS - PREP_T_KV

    def a_map(i):
        t = jnp.where(i < PREP_T_RET, i,
                      jnp.where(i < PREP_T_KV, i + src_ret,
                                jnp.where(i < PREP_T_GATE, i + src_kv, W_GATE // PREP_ROWS)))
        return (t, 0)

    def b_map(i):
        return (jnp.where((i >= PREP_T_RET) & (i < PREP_T_KV), i + src_ret + 1, 0), 0)

    return pl.pallas_call(
        _weight_prep_kernel,
        grid=(N_PROJ // PREP_ROWS,),
        in_specs=[pl.BlockSpec((PREP_ROWS, D), a_map), pl.BlockSpec((PREP_ROWS, D), b_map)],
        out_specs=pl.BlockSpec((PREP_ROWS, D), lambda i: (i, 0)),
        out_shape=jax.ShapeDtypeStruct((N_PROJ, D), BF16),
        compiler_params=pltpu.CompilerParams(
            dimension_semantics=("arbitrary",), vmem_limit_bytes=VMEM_LIMIT),
        name="weight_prep",
    )(w_t, w_t)


def _in_proj(x2, gain, w, colscale, tm=1024, tn=768):
    T, D = x2.shape
    N = w.shape[0]
    tm = min(tm, T)
    return pl.pallas_call(
        _inproj_kernel,
        grid=(T // tm, N // tn),
        in_specs=[
            pl.BlockSpec((tm, D), lambda i, j: (i, 0)),
            pl.BlockSpec((1, D), lambda i, j: (0, 0)),
            pl.BlockSpec((tn, D), lambda i, j: (j, 0)),
            pl.BlockSpec((1, tn), lambda i, j: (0, j)),
        ],
        out_specs=pl.BlockSpec((tm, tn), lambda i, j: (i, j)),
        out_shape=jax.ShapeDtypeStruct((T, N), BF16),
        scratch_shapes=[pltpu.VMEM((tm, D), BF16)],
        compiler_params=pltpu.CompilerParams(
            dimension_semantics=("parallel", "arbitrary"), vmem_limit_bytes=VMEM_LIMIT),
        name="in_proj",
    )(x2, gain, w, colscale)


def _relayout_kernel(cmp_ref, vs_ref, vw_ref, xc_ref, vt_ref, t_scr):
    tm = cmp_ref.shape[0]
    for s in range(4):
        t_scr[s] = cmp_ref[:, s * DK:(s + 1) * DK].astype(F32)
        for t in range(CMP_STRIDE):
            xc_ref[0, s, :, t * DK:(t + 1) * DK] = t_scr[
                s, pl.ds(t, tm // CMP_STRIDE, stride=CMP_STRIDE), :].astype(BF16)
    for kind, ref in enumerate((vs_ref, vw_ref)):
        v = ref[...].astype(F32)
        for g in range(NSA_G):
            for r in range(tm // QB):
                vt_ref[kind, 0, g, r] = v[r * QB:(r + 1) * QB, g * DK:(g + 1) * DK].T.astype(BF16)


def _relayout(proj, B, S, tm=1024):
    tm = min(tm, S)
    rpb = S // tm
    NC, NQ = S // CMP_STRIDE, S // QB
    return pl.pallas_call(
        _relayout_kernel,
        grid=(B * rpb,),
        in_specs=[
            pl.BlockSpec((tm, 4 * DK), lambda i: (i, COL_KV // (4 * DK))),
            pl.BlockSpec((tm, 2 * DK), lambda i: (i, (COL_KV + 6 * DK) // (2 * DK))),
            pl.BlockSpec((tm, 2 * DK), lambda i: (i, (COL_KV + 10 * DK) // (2 * DK))),
        ],
        out_specs=[
            pl.BlockSpec((1, 4, tm // CMP_STRIDE, CMP_STRIDE * DK), lambda i: (i // rpb, 0, i % rpb, 0)),
            pl.BlockSpec((2, 1, NSA_G, tm // QB, DK, QB), lambda i: (0, i // rpb, 0, i % rpb, 0, 0)),
        ],
        out_shape=[jax.ShapeDtypeStruct((B, 4, NC, CMP_STRIDE * DK), BF16),
                   jax.ShapeDtypeStruct((2, B, NSA_G, NQ, DK, QB), BF16)],
        scratch_shapes=[pltpu.VMEM((4, tm, DK), F32)],
        compiler_params=pltpu.CompilerParams(
            dimension_semantics=("parallel",), vmem_limit_bytes=VMEM_LIMIT),
        name="kv_relayout",
    )(proj, proj, proj)


def _compress_kernel(x_ref, pe_ref, w1_ref, b1_ref, w2_ref, o_ref):
    nc = x_ref.shape[2]
    half = x_ref.shape[3]
    xf = x_ref[0, 0].astype(F32)
    xa = (xf + pe_ref[0, 0:1, :]).astype(BF16)
    xb = (xf + pe_ref[0, 1:2, :]).astype(BF16)
    a = _dot(xa, w1_ref[0, :half, :])
    b = _dot(xb, w1_ref[0, half:, :])
    b_next = pltpu.roll(b, nc - 1, 0)
    hid = jax.nn.gelu(a + b_next + b1_ref[0])
    out = _dot(hid.astype(BF16), w2_ref[0])
    o_ref[0, 0, :CMP_PAD, :] = jnp.zeros((CMP_PAD, DK), F32)
    o_ref[0, 0, CMP_PAD:CMP_PAD + nc, :] = out
    o_ref[0, 0, CMP_PAD + nc:, :] = jnp.zeros((o_ref.shape[2] - CMP_PAD - nc, DK), F32)


def _compress(xc, pe, w1, b1, w2):
    B, _, NC, half = xc.shape
    rows = CMP_PAD + NC + 8
    return pl.pallas_call(
        _compress_kernel,
        grid=(B, 4),
        in_specs=[
            pl.BlockSpec((1, 1, NC, half), lambda b, s: (b, s, 0, 0)),
            pl.BlockSpec((1, 8, half), lambda b, s: (s // 2, 0, 0)),
            pl.BlockSpec((1, 2 * half, DK), lambda b, s: (s // 2, 0, 0)),
            pl.BlockSpec((1, 1, DK), lambda b, s: (s // 2, 0, 0)),
            pl.BlockSpec((1, DK, DK), lambda b, s: (s // 2, 0, 0)),
        ],
        out_specs=pl.BlockSpec((1, 1, rows, DK), lambda b, s: (b, s, 0, 0)),
        out_shape=jax.ShapeDtypeStruct((B, 4, rows, DK), F32),
        compiler_params=pltpu.CompilerParams(
            dimension_semantics=("parallel", "parallel"), vmem_limit_bytes=VMEM_LIMIT),
        name="compress_kv",
    )(xc, pe, w1, b1, w2)


TAB_NEAR, TAB_WIN, TAB_CMP, TAB_CONST, TAB_ROWS = 0, 256, 896, 1024, 1152
ONES_ROWS = 16
SEL_UNROLL = 4


def _nsa_kernel(q_ref, ksl_ref, vslt_ref, kw_ref, vwt_ref, gate_ref, kc_ref, vc_ref,
                msel_ref, e_ref, tab_ref, o_ref, acc_ref, m_ref, l_ref, out_ref, s_ref, p_ref, a_ref):
    qb = pl.program_id(1)
    groups = range(NSA_G)
    cols = NSA_HG * QB
    sub = lax.broadcasted_iota(jnp.int32, (128, cols), 0)
    qs_t, g_t = [], []
    for g in groups:
        qf = q_ref[:, g * cols:(g + 1) * cols].astype(F32)
        qs_t.append(jnp.concatenate([qf[:, h * DK:(h + 1) * DK].T for h in range(NSA_HG)],
                                    axis=1).astype(BF16))
        g_t.append(jax.nn.sigmoid(gate_ref[:, g * 128:(g + 1) * 128].astype(F32)).T)

    def gate_row(g, c):
        return jnp.concatenate([g_t[g][3 * h + c:3 * h + c + 1, :] for h in range(NSA_HG)], axis=1)

    for g in groups:
        m_ref[g] = jnp.full((1, cols), M_FLOOR, F32)
        l_ref[g] = jnp.zeros((1, cols), F32)
        acc_ref[g] = jnp.zeros((256, cols), F32)

    def cmp_chunk(g, start, bias):
        kc = kc_ref[0, g, pl.ds(start, 128), :].astype(BF16)
        vc_t = vc_ref[0, g, pl.ds(start, 128), :].T
        ms_t = msel_ref[pl.ds(start, 128), :].T
        s = _dot(kc, qs_t[g]) + bias
        m_old = m_ref[g]
        m_new = jnp.maximum(m_old, jnp.max(s, axis=0, keepdims=True))
        alpha = jnp.exp2(m_old - m_new)
        p = jnp.exp2(s - m_new)
        l_ref[g] = alpha * l_ref[g] + jnp.sum(p, axis=0, keepdims=True)
        lhs = jnp.concatenate([vc_t, ms_t], axis=0).astype(BF16)
        acc_ref[g] = alpha * acc_ref[g] + _dot(lhs, p.astype(BF16))
        m_ref[g] = m_new

    near_start = pl.multiple_of(8 * qb, 8)
    near_mask = jnp.where(sub + (8 * qb - CMP_PAD) < 0, NEG, 0.0)
    for g in groups:
        cmp_chunk(g, near_start, tab_ref[g, TAB_CMP:TAB_CMP + 128, :] + near_mask)

    def far_body(c, carry):
        st = pl.multiple_of(CMP_PAD + 128 * c, 8)
        far_mask = jnp.where(128 * c + sub < 8 * qb - CMP_PAD, 0.0, NEG)
        for g in groups:
            cmp_chunk(g, st, tab_ref[g, TAB_CONST:TAB_CONST + 128, :] + far_mask)
        return carry

    lax.fori_loop(0, (8 * qb + 7) // 128, far_body, 0)

    imp_t = []
    for g in groups:
        l = l_ref[g]
        inv = jnp.where(l > 0.0, 1.0 / l, 0.0)
        accv = acc_ref[g]
        out_ref[g] = accv[:DK] * (gate_row(g, 0) * inv)
        u = accv[DK:] * inv
        imp_t.append(u[:, 0:QB] + u[:, QB:2 * QB] + u[:, 2 * QB:3 * QB] + u[:, 3 * QB:4 * QB])

    tiles = [jnp.maximum(qb + d, 0) for d in range(-4, 1)]
    win_ones = jnp.ones((ONES_ROWS, 5 * QB), BF16)
    for g in groups:
        kwin = jnp.concatenate([kw_ref[pl.ds(pl.multiple_of(t * QB, QB), QB), g * DK:(g + 1) * DK]
                                for t in tiles], axis=0)
        vwin_t = jnp.concatenate([vwt_ref[0, 0, g, t] for t in tiles], axis=1)
        lhs = jnp.concatenate([vwin_t, win_ones], axis=0)
        bias = jnp.concatenate(
            [tab_ref[g, TAB_WIN + i * QB:TAB_WIN + (i + 1) * QB, :] + jnp.where(qb + d < 0, NEG, 0.0)
             for i, d in enumerate(range(-4, 1))], axis=0)
        s = _dot(kwin, qs_t[g]) + bias
        p = jnp.exp2(s - jnp.max(s, axis=0, keepdims=True))
        r = _dot(lhs, p.astype(BF16))
        out_ref[g] += gate_row(g, 2) * (r[:DK] / r[DK:DK + 1])

    ji = lax.broadcasted_iota(jnp.int32, (128, NSA_G * QB), 0)
    qi = lax.broadcasted_iota(jnp.int32, (128, NSA_G * QB), 1) & (QB - 1)
    cur = 2 * qb + (qi >= SEL_BLOCK).astype(jnp.int32)
    valid = ji <= cur
    forced = (ji == 0) | (ji == cur) | (ji == cur - 1)
    taken = -jnp.inf
    vt = jnp.where(forced, taken, jnp.where(valid, jnp.concatenate(imp_t, axis=1), -1.0))
    jio = ji.astype(F32)
    for _ in range(SEL_TOPN - 3):
        mx = jnp.max(vt, axis=0, keepdims=True)
        idx = jnp.min(jnp.where(vt == mx, jio, 128.0), axis=0, keepdims=True)
        vt = jnp.where(jio == idx, taken, vt)
    sb_all = jnp.where((vt == taken) & valid, 0.0, NEG)
    sb_far_all = jnp.where(ji < 2 * qb - 2, sb_all, NEG)
    q_far, q_near = [], []
    for g in groups:
        sb_t = sb_all[:, g * QB:(g + 1) * QB]
        sb_far = sb_far_all[:, g * QB:(g + 1) * QB]
        aug_far = (jnp.concatenate([sb_far] * NSA_HG, axis=1) + tab_ref[g, TAB_CONST:TAB_CONST + 128, :]).astype(BF16)
        aug_near = jnp.concatenate([sb_t] * NSA_HG, axis=1).astype(BF16)
        q_far.append(jnp.concatenate([qs_t[g], aug_far], axis=0))
        q_near.append(jnp.concatenate([qs_t[g], aug_near], axis=0))

    ones_t = jnp.ones((ONES_ROWS, 256), BF16)
    n_tiles = e_ref.shape[0] // 256
    arows = DK + ONES_ROWS

    def k_aug(g, t):
        st = pl.multiple_of(t * QB, QB)
        return jnp.concatenate([ksl_ref[pl.ds(st, QB), g * DK:(g + 1) * DK], e_ref[pl.ds(st, QB), :]], axis=1)

    def scores(g, t):
        tt = jnp.clip(t, 0, n_tiles - 1)
        return _dot(jnp.concatenate([k_aug(g, 2 * tt), k_aug(g, 2 * tt + 1)], axis=0), q_far[g])

    def softmax_stage(g, s, slot):
        m_old = m_ref[g]
        m_new = jnp.maximum(m_old, jnp.max(s, axis=0, keepdims=True))
        a_ref[g, slot] = jnp.exp2(m_old - m_new)
        p_ref[g, slot] = jnp.exp2(s - m_new).astype(BF16)
        m_ref[g] = m_new

    def values_stage(g, v_t, slot):
        lhs = jnp.concatenate([v_t, ones_t], axis=0)
        acc_ref[g, :arows] = a_ref[g, slot] * acc_ref[g, :arows] + _dot(lhs, p_ref[g, slot])

    def v_far(g, t):
        tt = jnp.clip(t, 0, n_tiles - 1)
        return jnp.concatenate([vslt_ref[0, 0, g, 2 * tt], vslt_ref[0, 0, g, 2 * tt + 1]], axis=1)

    n_far = qb // 2
    for g in groups:
        m_ref[g] = jnp.full((1, cols), M_FLOOR, F32)
        acc_ref[g] = jnp.zeros((256, cols), F32)
        s_ref[g, 0] = scores(g, 0)
        p_ref[g, 1] = jnp.zeros((256, cols), BF16)
        a_ref[g, 1] = jnp.ones((1, cols), F32)

    def sel_far_body(v, carry):
        a = SEL_UNROLL * v
        for k in range(SEL_UNROLL):
            cur_slot, other = k % 2, 1 - k % 2
            for g in groups:
                s_ref[g, other] = scores(g, a + k + 1)
                softmax_stage(g, s_ref[g, cur_slot], cur_slot)
                values_stage(g, v_far(g, a + k - 1), other)
        return carry

    n_iter = (n_far + SEL_UNROLL - 1) // SEL_UNROLL
    lax.fori_loop(0, n_iter, sel_far_body, 0)

    tp = jnp.maximum(qb - 1, 0)
    first_mask = jnp.where(qb == 0, NEG, 0.0)
    for g in groups:
        values_stage(g, v_far(g, SEL_UNROLL * n_iter - 1), 1)
        ka = jnp.concatenate([k_aug(g, tp), k_aug(g, qb)], axis=0)
        bias = jnp.concatenate([tab_ref[g, TAB_NEAR:TAB_NEAR + QB, :] + first_mask,
                                tab_ref[g, TAB_NEAR + QB:TAB_NEAR + 2 * QB, :]], axis=0)
        softmax_stage(g, _dot(ka, q_near[g]) + bias, 0)
        values_stage(g, jnp.concatenate([vslt_ref[0, 0, g, tp], vslt_ref[0, 0, g, qb]], axis=1), 0)
        accv = acc_ref[g, :arows]
        res = out_ref[g] + gate_row(g, 1) * (accv[:DK] / accv[DK:DK + 1])
        for h in range(NSA_HG):
            c0 = (g * NSA_HG + h) * DK
            o_ref[:, c0:c0 + DK] = res[:, h * QB:(h + 1) * QB].T.astype(o_ref.dtype)


def _nsa(proj, vt, kvc, msel, e_mat, tab, B, S):
    T = B * S
    NQ = S // QB
    cols = NSA_HG * QB
    rows_c = kvc.shape[2]
    once = pl.Buffered(1)
    seq = lambda c0: pl.BlockSpec((S, NSA_G * DK), lambda b, i: (b, c0 // (NSA_G * DK)), pipeline_mode=once)
    vt_spec = lambda kind: pl.BlockSpec((1, 1, NSA_G, NQ, DK, QB), lambda b, i: (kind, b, 0, 0, 0, 0),
                                        pipeline_mode=once)
    const = lambda a: pl.BlockSpec(a.shape, lambda b, i: (0,) * a.ndim, pipeline_mode=once)
    return pl.pallas_call(
        _nsa_kernel,
        grid=(B, NQ),
        in_specs=[
            pl.BlockSpec((QB, NSA_HEADS * DK), lambda b, i: (b * NQ + i, 0)),
            seq(COL_KV + 4 * DK), vt_spec(0), seq(COL_KV + 8 * DK), vt_spec(1),
            pl.BlockSpec((QB, NSA_G * 128), lambda b, i: (b * NQ + i, COL_GATE // (NSA_G * 128))),
            pl.BlockSpec((1, NSA_G, rows_c, DK), lambda b, i: (b, 0, 0, 0)),
            pl.BlockSpec((1, NSA_G, rows_c, DK), lambda b, i: (b, 1, 0, 0)),
            const(msel), const(e_mat), const(tab),
        ],
        out_specs=pl.BlockSpec((QB, NSA_HEADS * DK), lambda b, i: (b * NQ + i, 0)),
        out_shape=jax.ShapeDtypeStruct((T, NSA_HEADS * DK), BF16),
        scratch_shapes=[
            pltpu.VMEM((NSA_G, 256, cols), F32),
            pltpu.VMEM((NSA_G, 1, cols), F32),
            pltpu.VMEM((NSA_G, 1, cols), F32),
            pltpu.VMEM((NSA_G, DK, cols), F32),
            pltpu.VMEM((NSA_G, 2, 256, cols), F32),
            pltpu.VMEM((NSA_G, 2, 256, cols), BF16),
            pltpu.VMEM((NSA_G, 2, 1, cols), F32),
        ],
        compiler_params=pltpu.CompilerParams(
            dimension_semantics=("parallel", "arbitrary"), vmem_limit_bytes=VMEM_LIMIT),
        name="nsa_attention",
    )(proj, proj, vt, proj, vt, proj, kvc, kvc, msel, e_mat, tab)


def _retention_kernel(q_ref, k_ref, v_ref, g_ref, cos_ref, sin_ref, dec_ref, xz_ref, gc_ref, o_ref, r_scr):
    @pl.when(pl.program_id(1) == 0)
    def _():
        r_scr[...] = jnp.zeros(r_scr.shape, F32)

    cos = cos_ref[...]
    sin = sin_ref[...]
    hd = RET_D // 2

    def rot(x):
        x1, x2 = x[:, :hd], x[:, hd:]
        return jnp.concatenate([x1 * cos - x2 * sin, x2 * cos + x1 * sin], axis=1)

    for h in range(RET_HEADS):
        sl = slice(h * RET_D, (h + 1) * RET_D)
        qh = rot(q_ref[:, sl].astype(F32))
        kh = rot(k_ref[:, sl].astype(F32))
        vh = v_ref[:, sl]
        xi = xz_ref[h, 0]
        zeta = xz_ref[h, 1]
        qb16 = qh.astype(BF16)
        inner = _dot_nt(qb16, kh.astype(BF16)) * dec_ref[h]
        o = _dot(inner.astype(BF16), vh)
        r_old = r_scr[h]
        cross = _dot(qb16, r_old.astype(BF16))
        o = o + cross * jnp.concatenate([xi, xi], axis=1)
        kz = (kh * jnp.concatenate([zeta, zeta], axis=1)).astype(BF16)
        r_scr[h] = r_old * gc_ref[h] + _dot(kz.T, vh)
        mu = jnp.mean(o, axis=-1, keepdims=True)
        var = jnp.mean(jnp.square(o - mu), axis=-1, keepdims=True)
        on = (o - mu) * lax.rsqrt(var + GN_EPS)
        gf = g_ref[:, sl].astype(F32)
        o_ref[:, sl] = (gf * jax.nn.sigmoid(gf) * on).astype(o_ref.dtype)


def _retention(proj, cos, sin, decay, xz, gc, B, S):
    T = B * S
    N = S // RET_CHUNK
    W = RET_HEADS * RET_D
    sec = lambda c0: pl.BlockSpec((RET_CHUNK, W), lambda b, n: (b * N + n, c0 // W))
    return pl.pallas_call(
        _retention_kernel,
        grid=(B, N),
        in_specs=[
            sec(COL_QR), sec(COL_KR), sec(COL_VR), sec(COL_GR),
            pl.BlockSpec((RET_CHUNK, RET_D // 2), lambda b, n: (n, 0)),
            pl.BlockSpec((RET_CHUNK, RET_D // 2), lambda b, n: (n, 0)),
            pl.BlockSpec((RET_HEADS, RET_CHUNK, RET_CHUNK), lambda b, n: (0, 0, 0)),
            pl.BlockSpec((RET_HEADS, 2, RET_CHUNK, 128), lambda b, n: (0, 0, 0, 0)),
            pl.BlockSpec(memory_space=pltpu.SMEM),
        ],
        out_specs=pl.BlockSpec((RET_CHUNK, W), lambda b, n: (b * N + n, 0)),
        out_shape=jax.ShapeDtypeStruct((T, W), BF16),
        scratch_shapes=[pltpu.VMEM((RET_HEADS, RET_D, RET_D), F32)],
        compiler_params=pltpu.CompilerParams(
            dimension_semantics=("parallel", "arbitrary"), vmem_limit_bytes=VMEM_LIMIT),
        name="retention",
    )(proj, proj, proj, proj, cos, sin, decay, xz, gc)


def _rms(v, g):
    return v * lax.rsqrt(jnp.mean(v * v, axis=-1, keepdims=True) + NORM_EPS) * g


def _outproj_kernel(oa_ref, or_ref, w_ref, x_ref, gpost_ref, gpre_ref, x1_ref, h2_ref):
    wa = oa_ref.shape[1]
    mix = _dot(oa_ref[...], w_ref[:wa, :]) + _dot(or_ref[...], w_ref[wa:, :])
    x1 = x_ref[...] + _rms(mix, gpost_ref[...])
    x1_ref[...] = x1
    h2_ref[...] = _rms(x1, gpre_ref[...]).astype(h2_ref.dtype)


def _out_proj(oa, orr, w_out, x2, g_post, g_pre, tm=512):
    T, D = x2.shape
    tm = min(tm, T)
    row = lambda w: pl.BlockSpec((tm, w), lambda i: (i, 0))
    full = lambda a: pl.BlockSpec(a.shape, lambda i: (0,) * a.ndim)
    return pl.pallas_call(
        _outproj_kernel,
        grid=(T // tm,),
        in_specs=[row(oa.shape[1]), row(orr.shape[1]), full(w_out), row(D), full(g_post), full(g_pre)],
        out_specs=[row(D), row(D)],
        out_shape=[jax.ShapeDtypeStruct((T, D), F32), jax.ShapeDtypeStruct((T, D), BF16)],
        compiler_params=pltpu.CompilerParams(
            dimension_semantics=("parallel",), vmem_limit_bytes=VMEM_LIMIT),
        name="out_proj",
    )(oa, orr, w_out, x2, g_post, g_pre)


def _mlp_kernel(h_ref, wu_ref, wd_ref, x1_ref, g_ref, o_ref):
    f = pl.program_id(1)

    @pl.when(f == 0)
    def _():
        o_ref[...] = jnp.zeros(o_ref.shape, F32)

    u = jnp.maximum(_dot(h_ref[...], wu_ref[...]), 0.0)
    o_ref[...] += _dot((u * u).astype(BF16), wd_ref[...])

    @pl.when(f == pl.num_programs(1) - 1)
    def _():
        o_ref[...] = x1_ref[...] + _rms(o_ref[...], g_ref[...])


def _mlp(h2, w_up, w_down, x1, g_post, tm=1024, tf=512):
    T, D = x1.shape
    F = w_up.shape[1]
    tm = min(tm, T)
    return pl.pallas_call(
        _mlp_kernel,
        grid=(T // tm, F // tf),
        in_specs=[
            pl.BlockSpec((tm, D), lambda i, f: (i, 0)),
            pl.BlockSpec((D, tf), lambda i, f: (0, f)),
            pl.BlockSpec((tf, D), lambda i, f: (f, 0)),
            pl.BlockSpec((tm, D), lambda i, f: (i, 0), pipeline_mode=pl.Buffered(1)),
            pl.BlockSpec((1, D), lambda i, f: (0, 0)),
        ],
        out_specs=pl.BlockSpec((tm, D), lambda i, f: (i, 0)),
        out_shape=jax.ShapeDtypeStruct((T, D), F32),
        compiler_params=pltpu.CompilerParams(
            dimension_semantics=("parallel", "arbitrary"), vmem_limit_bytes=VMEM_LIMIT),
        name="mlp",
    )(h2, w_up, w_down, x1, g_post)


def _t5_bucket_np(rel):
    n = np.maximum(rel, 0)
    max_exact = T5_BUCKETS // 2
    nf = np.maximum(n, 1).astype(np.float64)
    large = max_exact + (np.log(nf / max_exact) / np.log(T5_MAX_DIST / max_exact)
                         * (T5_BUCKETS - max_exact)).astype(np.int64)
    large = np.minimum(large, T5_BUCKETS - 1)
    return np.where(n < max_exact, n, large)


@functools.lru_cache(maxsize=None)
def _static_tables(S):
    masked = T5_BUCKETS
    i = np.arange(QB)[None, :]

    def bucket_rows(rel, valid):
        return np.where(valid, _t5_bucket_np(rel), masked)

    kk = np.arange(256)[:, None]
    rel = i - kk + QB
    b_near = bucket_rows(rel, rel >= 0)
    kk = np.arange(640)[:, None]
    rel = i - kk + WINDOW
    b_win = bucket_rows(rel, (rel >= 0) & (rel < WINDOW))
    m = np.arange(128)[:, None]
    rel = i - CMP_STRIDE * (m - CMP_PAD) - (CMP_BLOCK - 1)
    b_cmp = bucket_rows(rel, rel >= 0)
    b_const = np.full((128, QB), T5_BUCKETS - 1)
    bucket_idx = np.concatenate([b_near, b_win, b_cmp, b_const], axis=0).astype(np.int32)
    n_cmp = (S - CMP_BLOCK) // CMP_STRIDE + 1
    n_sel = S // SEL_BLOCK
    cs = np.arange(n_cmp) * CMP_STRIDE
    ss = np.arange(n_sel) * SEL_BLOCK
    overlap = (cs[:, None] <= ss[None, :] + SEL_BLOCK - 1) & (cs[:, None] + CMP_BLOCK - 1 >= ss[None, :])
    rows = CMP_PAD + S // CMP_STRIDE + 8
    msel = np.zeros((rows, 128), np.float32)
    msel[CMP_PAD:CMP_PAD + n_cmp, :n_sel] = overlap
    e_mat = (np.arange(S)[:, None] // SEL_BLOCK == np.arange(128)[None, :]).astype(np.float32)
    log_gamma = np.log(1.0 - np.exp2(-5.0 - np.arange(RET_HEADS, dtype=np.float32))).astype(np.float32)
    idx = np.arange(RET_CHUNK, dtype=np.float32)
    diff = idx[:, None] - idx[None, :]
    decay = np.where(diff[None] >= 0, np.exp(np.maximum(diff, 0.0)[None] * log_gamma[:, None, None]), 0.0)
    xi = np.exp((idx + 1.0)[None, :] * log_gamma[:, None])
    zeta = np.exp((RET_CHUNK - 1.0 - idx)[None, :] * log_gamma[:, None])
    xz = np.stack([np.broadcast_to(xi[:, :, None], (RET_HEADS, RET_CHUNK, 128)),
                   np.broadcast_to(zeta[:, :, None], (RET_HEADS, RET_CHUNK, 128))], axis=1)
    g_c = np.exp(RET_CHUNK * log_gamma)
    inv_freq = ROPE_BASE ** (-np.arange(0, RET_D, 2, dtype=np.float32) / RET_D)
    ang = np.arange(S, dtype=np.float32)[:, None] * inv_freq[None, :]
    return dict(bucket_idx=bucket_idx, msel=msel, e_mat=e_mat, decay=decay.astype(np.float32),
                xz=xz.astype(np.float32), g_c=g_c.astype(np.float32), ang=ang.astype(np.float32))


def _bias_table(t5_bias, st):
    assert st["bucket_idx"].shape == (TAB_ROWS, QB)
    idx = jnp.asarray(st["bucket_idx"].reshape(-1))
    onehot = (idx[None, :] == jnp.arange(T5_BUCKETS + 1, dtype=jnp.int32)[:, None]).astype(F32)
    vals = jnp.concatenate([t5_bias.astype(F32) * LOG2E, jnp.full((1, NSA_HEADS), NEG, F32)], axis=0)
    tab = jnp.einsum("bh,bn->hn", vals, onehot, precision=lax.Precision.HIGHEST)
    tab = tab.reshape(NSA_G, NSA_HG, TAB_ROWS, QB).transpose(0, 2, 1, 3)
    return tab.reshape(NSA_G, TAB_ROWS, NSA_HG * QB)


def kernel(x, norm_mix_pre, w_in, cmp_pe_k, cmp_w1_k, cmp_b1_k, cmp_w2_k, cmp_pe_v, cmp_w1_v, cmp_b1_v,
           cmp_w2_v, t5_bias, w_out, norm_mix_post, norm_mlp_pre, w_up, w_down, norm_mlp_post):
    B, S, D = x.shape
    T = B * S
    depth = w_in.shape[0]
    st = _static_tables(S)
    tab = _bias_table(t5_bias, st)
    msel = jnp.asarray(st["msel"])
    e_mat = jnp.asarray(st["e_mat"], BF16)
    decay = jnp.asarray(st["decay"])
    xz = jnp.asarray(st["xz"])
    gc = jnp.asarray(st["g_c"])
    ang = jnp.asarray(st["ang"])
    cos, sin = jnp.cos(ang), jnp.sin(ang)

    colscale = np.ones((1, N_PROJ), np.float32)
    colscale[0, COL_QA:COL_QA + NSA_HEADS * DK] = DK ** -0.5 * LOG2E
    colscale[0, COL_KR:COL_KR + RET_HEADS * RET_D] = RET_D ** -0.5
    colscale = jnp.asarray(colscale)

    xcur = x.reshape(T, D)
    for l in range(depth):
        w_re = _weight_prep(jnp.transpose(w_in[l]))
        proj = _in_proj(xcur, norm_mix_pre[l][None, :], w_re, colscale)

        xc, vt = _relayout(proj, B, S)
        half = CMP_STRIDE * DK
        pe = jnp.stack([cmp_pe_k[l].reshape(2, half), cmp_pe_v[l].reshape(2, half)])
        pe = jnp.concatenate([pe, jnp.zeros((2, 6, half), F32)], axis=1)
        w1 = jnp.stack([cmp_w1_k[l], cmp_w1_v[l]]).astype(BF16)
        b1 = jnp.stack([cmp_b1_k[l], cmp_b1_v[l]])[:, None, :]
        w2 = jnp.stack([cmp_w2_k[l], cmp_w2_v[l]]).astype(BF16)
        kvc = _compress(xc, pe, w1, b1, w2)

        o_a = _nsa(proj, vt, kvc, msel, e_mat, tab, B, S)
        o_r = _retention(proj, cos, sin, decay, xz, gc, B, S)

        x1, h2 = _out_proj(o_a, o_r, w_out[l].astype(BF16), xcur,
                           norm_mix_post[l][None, :], norm_mlp_pre[l][None, :])
        xcur = _mlp(h2, w_up[l].astype(BF16), w_down[l].astype(BF16), x1, norm_mlp_post[l][None, :])
    return xcur.reshape(B, S, D)
```

```python
import functools

import numpy as np
import jax
import jax.numpy as jnp
from jax import lax
from jax.experimental import pallas as pl
from jax.experimental.pallas import tpu as pltpu

F32 = jnp.float32
BF16 = jnp.bfloat16

D_MODEL = 2048
NSA_HEADS = 8
NSA_G = 2
NSA_HG = 4
DK = 128
CMP_BLOCK = 32
CMP_STRIDE = 16
SEL_BLOCK = 64
SEL_TOPN = 16
WINDOW = 512
QB = 128
RET_HEADS = 4
RET_D = 256
RET_CHUNK = 128
ROPE_BASE = 10000.0
GN_EPS = 1e-6
NORM_EPS = 1e-6
T5_BUCKETS = 32
T5_MAX_DIST = 128
D_FF = 4 * D_MODEL
FORCE = 1e4
NEG = -1e30
M_FLOOR = -1e20
LOG2E = 1.4426950408889634

COL_QA = 0
COL_QR = 1024
COL_KR = 2048
COL_VR = 3072
COL_GR = 4096
COL_KV = 5120
COL_GATE = 6656
N_PROJ = 6912
W_KV = NSA_HEADS * DK
W_GATE = W_KV + 6 * NSA_G * DK
W_RET = W_GATE + 3 * NSA_HEADS
CMP_PAD = 120

VMEM_LIMIT = 56 * 1024 * 1024


def _dot(a, b):
    return jnp.dot(a, b, preferred_element_type=F32)


def _dot_nt(a, b):
    return lax.dot_general(a, b, (((1,), (1,)), ((), ())), preferred_element_type=F32)


def _inproj_kernel(x_ref, g_ref, w_ref, cs_ref, o_ref, h_scr):
    @pl.when(pl.program_id(1) == 0)
    def _():
        xf = x_ref[...]
        ms = jnp.mean(xf * xf, axis=-1, keepdims=True)
        h_scr[...] = (xf * lax.rsqrt(ms + NORM_EPS) * g_ref[...]).astype(BF16)

    acc = _dot_nt(h_scr[...], w_ref[...])
    o_ref[...] = (acc * cs_ref[...]).astype(o_ref.dtype)


PREP_ROWS = 256
PREP_T_RET = COL_QR // PREP_ROWS
PREP_T_KV = COL_KV // PREP_ROWS
PREP_T_GATE = COL_GATE // PREP_ROWS
PREP_SHIFT = W_RET % PREP_ROWS


def _weight_prep_kernel(a_ref, b_ref, o_ref):
    i = pl.program_id(0)
    aligned = (i < PREP_T_RET) | ((i >= PREP_T_KV) & (i < PREP_T_GATE))

    @pl.when(aligned)
    def _():
        o_ref[...] = a_ref[...].astype(BF16)

    @pl.when((i >= PREP_T_RET) & (i < PREP_T_KV))
    def _():
        o_ref[...] = jnp.concatenate([a_ref[PREP_SHIFT:, :], b_ref[:PREP_SHIFT, :]], axis=0).astype(BF16)

    @pl.when(i == PREP_T_GATE)
    def _():
        ng = 3 * NSA_HG
        a = a_ref[:128, :]
        row = lax.broadcasted_iota(jnp.int32, a.shape, 0)
        g0 = jnp.where(row < ng, a, 0.0)
        g1 = jnp.where(row < ng, pltpu.roll(a, 128 - ng, 0), 0.0)
        o_ref[...] = jnp.concatenate([g0, g1], axis=0).astype(BF16)


def _weight_prep(w_t):
    n_in, D = w_t.shape
    assert W_KV % PREP_ROWS == 0 and W_GATE % PREP_ROWS == 0 and N_PROJ - COL_GATE == PREP_ROWS
    src_ret = W_RET // PREP_ROWS - PREP_T_RET
    src_kv = W_KV // PREP_ROWS - PREP_T_KV

    def a_map(i):
        t = jnp.where(i < PREP_T_RET, i,
                      jnp.where(i < PREP_T_KV, i + src_ret,
                                jnp.where(i < PREP_T_GATE, i + src_kv, W_GATE // PREP_ROWS)))
        return (t, 0)

    def b_map(i):
        return (jnp.where((i >= PREP_T_RET) & (i < PREP_T_KV), i + src_ret + 1, 0), 0)

    return pl.pallas_call(
        _weight_prep_kernel,
        grid=(N_PROJ // PREP_ROWS,),
        in_specs=[pl.BlockSpec((PREP_ROWS, D), a_map), pl.BlockSpec((PREP_ROWS, D), b_map)],
        out_specs=pl.BlockSpec((PREP_ROWS, D), lambda i: (i, 0)),
        out_shape=jax.ShapeDtypeStruct((N_PROJ, D), BF16),
        compiler_params=pltpu.CompilerParams(
            dimension_semantics=("arbitrary",), vmem_limit_bytes=VMEM_LIMIT),
        name="weight_prep",
    )(w_t, w_t)


def _in_proj(x2, gain, w, colscale, tm=1024, tn=768):
    T, D = x2.shape
    N = w.shape[0]
    tm = min(tm, T)
    return pl.pallas_call(
        _inproj_kernel,
        grid=(T // tm, N // tn),
        in_specs=[
            pl.BlockSpec((tm, D), lambda i, j: (i, 0)),
            pl.BlockSpec((1, D), lambda i, j: (0, 0)),
            pl.BlockSpec((tn, D), lambda i, j: (j, 0)),
            pl.BlockSpec((1, tn), lambda i, j: (0, j)),
        ],
        out_specs=pl.BlockSpec((tm, tn), lambda i, j: (i, j)),
        out_shape=jax.ShapeDtypeStruct((T, N), BF16),
        scratch_shapes=[pltpu.VMEM((tm, D), BF16)],
        compiler_params=pltpu.CompilerParams(
            dimension_semantics=("parallel", "arbitrary"), vmem_limit_bytes=VMEM_LIMIT),
        name="in_proj",
    )(x2, gain, w, colscale)


def _relayout_kernel(cmp_ref, vs_ref, vw_ref, xc_ref, vt_ref, t_scr):
    tm = cmp_ref.shape[0]
    for s in range(4):
        t_scr[s] = cmp_ref[:, s * DK:(s + 1) * DK].astype(F32)
        for t in range(CMP_STRIDE):
            xc_ref[0, s, :, t * DK:(t + 1) * DK] = t_scr[
                s, pl.ds(t, tm // CMP_STRIDE, stride=CMP_STRIDE), :].astype(BF16)
    for kind, ref in enumerate((vs_ref, vw_ref)):
        v = ref[...].astype(F32)
        for g in range(NSA_G):
            for r in range(tm // QB):
                vt_ref[kind, 0, g, r] = v[r * QB:(r + 1) * QB, g * DK:(g + 1) * DK].T.astype(BF16)


def _relayout(proj, B, S, tm=1024):
    tm = min(tm, S)
    rpb = S // tm
    NC, NQ = S // CMP_STRIDE, S // QB
    return pl.pallas_call(
        _relayout_kernel,
        grid=(B * rpb,),
        in_specs=[
            pl.BlockSpec((tm, 4 * DK), lambda i: (i, COL_KV // (4 * DK))),
            pl.BlockSpec((tm, 2 * DK), lambda i: (i, (COL_KV + 6 * DK) // (2 * DK))),
            pl.BlockSpec((tm, 2 * DK), lambda i: (i, (COL_KV + 10 * DK) // (2 * DK))),
        ],
        out_specs=[
            pl.BlockSpec((1, 4, tm // CMP_STRIDE, CMP_STRIDE * DK), lambda i: (i // rpb, 0, i % rpb, 0)),
            pl.BlockSpec((2, 1, NSA_G, tm // QB, DK, QB), lambda i: (0, i // rpb, 0, i % rpb, 0, 0)),
        ],
        out_shape=[jax.ShapeDtypeStruct((B, 4, NC, CMP_STRIDE * DK), BF16),
                   jax.ShapeDtypeStruct((2, B, NSA_G, NQ, DK, QB), BF16)],
        scratch_shapes=[pltpu.VMEM((4, tm, DK), F32)],
        compiler_params=pltpu.CompilerParams(
            dimension_semantics=("parallel",), vmem_limit_bytes=VMEM_LIMIT),
        name="kv_relayout",
    )(proj, proj, proj)


def _compress_kernel(x_ref, pe_ref, w1_ref, b1_ref, w2_ref, o_ref):
    nc = x_ref.shape[2]
    half = x_ref.shape[3]
    xf = x_ref[0, 0].astype(F32)
    xa = (xf + pe_ref[0, 0:1, :]).astype(BF16)
    xb = (xf + pe_ref[0, 1:2, :]).astype(BF16)
    a = _dot(xa, w1_ref[0, :half, :])
    b = _dot(xb, w1_ref[0, half:, :])
    b_next = pltpu.roll(b, nc - 1, 0)
    hid = jax.nn.gelu(a + b_next + b1_ref[0])
    out = _dot(hid.astype(BF16), w2_ref[0])
    o_ref[0, 0, :CMP_PAD, :] = jnp.zeros((CMP_PAD, DK), F32)
    o_ref[0, 0, CMP_PAD:CMP_PAD + nc, :] = out
    o_ref[0, 0, CMP_PAD + nc:, :] = jnp.zeros((o_ref.shape[2] - CMP_PAD - nc, DK), F32)


def _compress(xc, pe, w1, b1, w2):
    B, _, NC, half = xc.shape
    rows = CMP_PAD + NC + 8
    return pl.pallas_call(
        _compress_kernel,
        grid=(B, 4),
        in_specs=[
            pl.BlockSpec((1, 1, NC, half), lambda b, s: (b, s, 0, 0)),
            pl.BlockSpec((1, 8, half), lambda b, s: (s // 2, 0, 0)),
            pl.BlockSpec((1, 2 * half, DK), lambda b, s: (s // 2, 0, 0)),
            pl.BlockSpec((1, 1, DK), lambda b, s: (s // 2, 0, 0)),
            pl.BlockSpec((1, DK, DK), lambda b, s: (s // 2, 0, 0)),
        ],
        out_specs=pl.BlockSpec((1, 1, rows, DK), lambda b, s: (b, s, 0, 0)),
        out_shape=jax.ShapeDtypeStruct((B, 4, rows, DK), F32),
        compiler_params=pltpu.CompilerParams(
            dimension_semantics=("parallel", "parallel"), vmem_limit_bytes=VMEM_LIMIT),
        name="compress_kv",
    )(xc, pe, w1, b1, w2)


TAB_NEAR, TAB_WIN, TAB_CMP, TAB_CONST, TAB_ROWS = 0, 256, 896, 1024, 1152
ONES_ROWS = 16
SEL_UNROLL = 4


def _nsa_kernel(q_ref, ksl_ref, vslt_ref, kw_ref, vwt_ref, gate_ref, kc_ref, vc_ref,
                msel_ref, e_ref, tab_ref, rq_ref, rk_ref, rv_ref, rg_ref, cos_ref, sin_ref, dec_ref, xz_ref, gc_ref,
                o_ref, or_ref, acc_ref, m_ref, l_ref, out_ref, p_ref, a_ref, r_scr):
    qb = pl.program_id(1)
    groups = range(NSA_G)

    @pl.when(qb == 0)
    def _():
        r_scr[...] = jnp.zeros(r_scr.shape, F32)

    retention = functools.partial(
        _retention_heads, q_ref=rq_ref, k_ref=rk_ref, v_ref=rv_ref, g_ref=rg_ref, cos_ref=cos_ref, sin_ref=sin_ref,
        dec_ref=dec_ref, xz_ref=xz_ref, gc_ref=gc_ref, o_ref=or_ref, r_scr=r_scr)
    cols = NSA_HG * QB
    sub = lax.broadcasted_iota(jnp.int32, (128, cols), 0)
    qs_t, g_t = [], []
    for g in groups:
        qf = q_ref[:, g * cols:(g + 1) * cols].astype(F32)
        qs_t.append(jnp.concatenate([qf[:, h * DK:(h + 1) * DK].T for h in range(NSA_HG)],
                                    axis=1).astype(BF16))
        g_t.append(jax.nn.sigmoid(gate_ref[:, g * 128:(g + 1) * 128].astype(F32)).T)

    def gate_row(g, c):
        return jnp.concatenate([g_t[g][3 * h + c:3 * h + c + 1, :] for h in range(NSA_HG)], axis=1)

    retention(range(0, RET_HEADS // 2))

    for g in groups:
        m_ref[g] = jnp.full((1, cols), M_FLOOR, F32)
        l_ref[g] = jnp.zeros((1, cols), F32)
        acc_ref[g] = jnp.zeros((256, cols), F32)

    def cmp_chunk(g, start, bias):
        kc = kc_ref[0, g, pl.ds(start, 128), :].astype(BF16)
        vc_t = vc_ref[0, g, pl.ds(start, 128), :].T
        ms_t = msel_ref[pl.ds(start, 128), :].T
        s = _dot(kc, qs_t[g]) + bias
        m_old = m_ref[g]
        m_new = jnp.maximum(m_old, jnp.max(s, axis=0, keepdims=True))
        alpha = jnp.exp2(m_old - m_new)
        p = jnp.exp2(s - m_new)
        l_ref[g] = alpha * l_ref[g] + jnp.sum(p, axis=0, keepdims=True)
        lhs = jnp.concatenate([vc_t, ms_t], axis=0).astype(BF16)
        acc_ref[g] = alpha * acc_ref[g] + _dot(lhs, p.astype(BF16))
        m_ref[g] = m_new

    near_start = pl.multiple_of(8 * qb, 8)
    near_mask = jnp.where(sub + (8 * qb - CMP_PAD) < 0, NEG, 0.0)
    for g in groups:
        cmp_chunk(g, near_start, tab_ref[g, TAB_CMP:TAB_CMP + 128, :] + near_mask)

    def far_body(c, carry):
        st = pl.multiple_of(CMP_PAD + 128 * c, 8)
        far_mask = jnp.where(128 * c + sub < 8 * qb - CMP_PAD, 0.0, NEG)
        for g in groups:
            cmp_chunk(g, st, tab_ref[g, TAB_CONST:TAB_CONST + 128, :] + far_mask)
        return carry

    lax.fori_loop(0, (8 * qb + 7) // 128, far_body, 0)

    imp_t = []
    for g in groups:
        l = l_ref[g]
        inv = jnp.where(l > 0.0, 1.0 / l, 0.0)
        accv = acc_ref[g]
        out_ref[g] = accv[:DK] * (gate_row(g, 0) * inv)
        u = accv[DK:] * inv
        imp_t.append(u[:, 0:QB] + u[:, QB:2 * QB] + u[:, 2 * QB:3 * QB] + u[:, 3 * QB:4 * QB])

    tiles = [jnp.maximum(qb + d, 0) for d in range(-4, 1)]
    win_ones = jnp.ones((ONES_ROWS, 5 * QB), BF16)
    for g in groups:
        kwin = jnp.concatenate([kw_ref[pl.ds(pl.multiple_of(t * QB, QB), QB), g * DK:(g + 1) * DK]
                                for t in tiles], axis=0)
        vwin_t = jnp.concatenate([vwt_ref[0, 0, g, t] for t in tiles], axis=1)
        lhs = jnp.concatenate([vwin_t, win_ones], axis=0)
        bias = jnp.concatenate(
            [tab_ref[g, TAB_WIN + i * QB:TAB_WIN + (i + 1) * QB, :] + jnp.where(qb + d < 0, NEG, 0.0)
             for i, d in enumerate(range(-4, 1))], axis=0)
        s = _dot(kwin, qs_t[g]) + bias
        p = jnp.exp2(s - jnp.max(s, axis=0, keepdims=True))
        r = _dot(lhs, p.astype(BF16))
        out_ref[g] += gate_row(g, 2) * (r[:DK] / r[DK:DK + 1])

    ji = lax.broadcasted_iota(jnp.int32, (128, NSA_G * QB), 0)
    qi = lax.broadcasted_iota(jnp.int32, (128, NSA_G * QB), 1) & (QB - 1)
    cur = 2 * qb + (qi >= SEL_BLOCK).astype(jnp.int32)
    valid = ji <= cur
    forced = (ji == 0) | (ji == cur) | (ji == cur - 1)
    taken = -jnp.inf
    vt = jnp.where(forced, taken, jnp.where(valid, jnp.concatenate(imp_t, axis=1), -1.0))
    jio = ji.astype(F32)
    for _ in range(SEL_TOPN - 3):
        mx = jnp.max(vt, axis=0, keepdims=True)
        idx = jnp.min(jnp.where(vt == mx, jio, 128.0), axis=0, keepdims=True)
        vt = jnp.where(jio == idx, taken, vt)
    sb_all = jnp.where((vt == taken) & valid, 0.0, NEG)
    sb_far_all = jnp.where(ji < 2 * qb - 2, sb_all, NEG)
    q_far, q_near = [], []
    for g in groups:
        sb_t = sb_all[:, g * QB:(g + 1) * QB]
        sb_far = sb_far_all[:, g * QB:(g + 1) * QB]
        aug_far = (jnp.concatenate([sb_far] * NSA_HG, axis=1) + tab_ref[g, TAB_CONST:TAB_CONST + 128, :]).astype(BF16)
        aug_near = jnp.concatenate([sb_t] * NSA_HG, axis=1).astype(BF16)
        q_far.append(jnp.concatenate([qs_t[g], aug_far], axis=0))
        q_near.append(jnp.concatenate([qs_t[g], aug_near], axis=0))

    ones_t = jnp.ones((ONES_ROWS, 256), BF16)
    n_tiles = e_ref.shape[0] // 256
    arows = DK + ONES_ROWS

    def k_aug(g, t):
        st = pl.multiple_of(t * QB, QB)
        return jnp.concatenate([ksl_ref[pl.ds(st, QB), g * DK:(g + 1) * DK], e_ref[pl.ds(st, QB), :]], axis=1)

    def scores(g, t):
        tt = jnp.clip(t, 0, n_tiles - 1)
        return _dot(jnp.concatenate([k_aug(g, 2 * tt), k_aug(g, 2 * tt + 1)], axis=0), q_far[g])

    def softmax_stage(g, s, slot):
        m_old = m_ref[g]
        m_new = jnp.maximum(m_old, jnp.max(s, axis=0, keepdims=True))
        a_ref[g, slot] = jnp.exp2(m_old - m_new)
        p_ref[g, slot] = jnp.exp2(s - m_new).astype(BF16)
        m_ref[g] = m_new

    def values_stage(g, v_t, slot):
        lhs = jnp.concatenate([v_t, ones_t], axis=0)
        acc_ref[g, :arows] = a_ref[g, slot] * acc_ref[g, :arows] + _dot(lhs, p_ref[g, slot])

    def v_far(g, t):
        tt = jnp.clip(t, 0, n_tiles - 1)
        return jnp.concatenate([vslt_ref[0, 0, g, 2 * tt], vslt_ref[0, 0, g, 2 * tt + 1]], axis=1)

    n_far = qb // 2
    for g in groups:
        m_ref[g] = jnp.full((1, cols), M_FLOOR, F32)
        acc_ref[g] = jnp.zeros((256, cols), F32)
        softmax_stage(g, scores(g, 0), 0)

    def sel_far_body(v, carry):
        a = SEL_UNROLL * v
        for k in range(SEL_UNROLL):
            cur_slot, other = k % 2, 1 - k % 2
            for g in groups:
                softmax_stage(g, scores(g, a + k + 1), other)
                values_stage(g, v_far(g, a + k), cur_slot)
        return carry

    n_iter = (n_far + SEL_UNROLL - 1) // SEL_UNROLL
    lax.fori_loop(0, n_iter, sel_far_body, 0)

    retention(range(RET_HEADS // 2, RET_HEADS))
    tp = jnp.maximum(qb - 1, 0)
    first_mask = jnp.where(qb == 0, NEG, 0.0)
    for g in groups:
        ka = jnp.concatenate([k_aug(g, tp), k_aug(g, qb)], axis=0)
        bias = jnp.concatenate([tab_ref[g, TAB_NEAR:TAB_NEAR + QB, :] + first_mask,
                                tab_ref[g, TAB_NEAR + QB:TAB_NEAR + 2 * QB, :]], axis=0)
        softmax_stage(g, _dot(ka, q_near[g]) + bias, 0)
        values_stage(g, jnp.concatenate([vslt_ref[0, 0, g, tp], vslt_ref[0, 0, g, qb]], axis=1), 0)
        accv = acc_ref[g, :arows]
        res = out_ref[g] + gate_row(g, 1) * (accv[:DK] / accv[DK:DK + 1])
        for h in range(NSA_HG):
            c0 = (g * NSA_HG + h) * DK
            o_ref[:, c0:c0 + DK] = res[:, h * QB:(h + 1) * QB].T.astype(o_ref.dtype)


def _token_mix(proj, vt, kvc, msel, e_mat, tab, cos, sin, decay, xz, gc, B, S):
    T = B * S
    NQ = S // QB
    cols = NSA_HG * QB
    rows_c = kvc.shape[2]
    once = pl.Buffered(1)
    seq = lambda c0: pl.BlockSpec((S, NSA_G * DK), lambda b, i: (b, c0 // (NSA_G * DK)), pipeline_mode=once)
    vt_spec = lambda kind: pl.BlockSpec((1, 1, NSA_G, NQ, DK, QB), lambda b, i: (kind, b, 0, 0, 0, 0),
                                        pipeline_mode=once)
    const = lambda a: pl.BlockSpec(a.shape, lambda b, i: (0,) * a.ndim, pipeline_mode=once)
    ret_w = RET_HEADS * RET_D
    assert RET_CHUNK == QB
    ret_sec = lambda c0: pl.BlockSpec((RET_CHUNK, ret_w), lambda b, i: (b * NQ + i, c0 // ret_w))
    return pl.pallas_call(
        _nsa_kernel,
        grid=(B, NQ),
        in_specs=[
            pl.BlockSpec((QB, NSA_HEADS * DK), lambda b, i: (b * NQ + i, 0)),
            seq(COL_KV + 4 * DK), vt_spec(0), seq(COL_KV + 8 * DK), vt_spec(1),
            pl.BlockSpec((QB, NSA_G * 128), lambda b, i: (b * NQ + i, COL_GATE // (NSA_G * 128))),
            pl.BlockSpec((1, NSA_G, rows_c, DK), lambda b, i: (b, 0, 0, 0)),
            pl.BlockSpec((1, NSA_G, rows_c, DK), lambda b, i: (b, 1, 0, 0)),
            const(msel), const(e_mat), const(tab),
            ret_sec(COL_QR), ret_sec(COL_KR), ret_sec(COL_VR), ret_sec(COL_GR),
            pl.BlockSpec((RET_CHUNK, RET_D // 2), lambda b, i: (i, 0)),
            pl.BlockSpec((RET_CHUNK, RET_D // 2), lambda b, i: (i, 0)),
            const(decay), const(xz),
            pl.BlockSpec(memory_space=pltpu.SMEM),
        ],
        out_specs=[pl.BlockSpec((QB, NSA_HEADS * DK), lambda b, i: (b * NQ + i, 0)),
                   pl.BlockSpec((RET_CHUNK, ret_w), lambda b, i: (b * NQ + i, 0))],
        out_shape=[jax.ShapeDtypeStruct((T, NSA_HEADS * DK), BF16),
                   jax.ShapeDtypeStruct((T, ret_w), BF16)],
        scratch_shapes=[
            pltpu.VMEM((NSA_G, 256, cols), F32),
            pltpu.VMEM((NSA_G, 1, cols), F32),
            pltpu.VMEM((NSA_G, 1, cols), F32),
            pltpu.VMEM((NSA_G, DK, cols), F32),
            pltpu.VMEM((NSA_G, 2, 256, cols), BF16),
            pltpu.VMEM((NSA_G, 2, 1, cols), F32),
            pltpu.VMEM((RET_HEADS, RET_D, RET_D), F32),
        ],
        compiler_params=pltpu.CompilerParams(
            dimension_semantics=("parallel", "arbitrary"), vmem_limit_bytes=VMEM_LIMIT),
        name="token_mix",
    )(proj, proj, vt, proj, vt, proj, kvc, kvc, msel, e_mat, tab, proj, proj, proj, proj, cos, sin, decay, xz, gc)


def _retention_heads(heads, q_ref, k_ref, v_ref, g_ref, cos_ref, sin_ref, dec_ref, xz_ref, gc_ref, o_ref, r_scr):
    cos = cos_ref[...]
    sin = sin_ref[...]
    hd = RET_D // 2

    def rot(x):
        x1, x2 = x[:, :hd], x[:, hd:]
        return jnp.concatenate([x1 * cos - x2 * sin, x2 * cos + x1 * sin], axis=1)

    for h in heads:
        sl = slice(h * RET_D, (h + 1) * RET_D)
        qh = rot(q_ref[:, sl].astype(F32))
        kh = rot(k_ref[:, sl].astype(F32))
        vh = v_ref[:, sl]
        xi = xz_ref[h, 0]
        zeta = xz_ref[h, 1]
        qb16 = qh.astype(BF16)
        inner = _dot_nt(qb16, kh.astype(BF16)) * dec_ref[h]
        o = _dot(inner.astype(BF16), vh)
        r_old = r_scr[h]
        cross = _dot(qb16, r_old.astype(BF16))
        o = o + cross * jnp.concatenate([xi, xi], axis=1)
        kz = (kh * jnp.concatenate([zeta, zeta], axis=1)).astype(BF16)
        r_scr[h] = r_old * gc_ref[h] + _dot(kz.T, vh)
        mu = jnp.mean(o, axis=-1, keepdims=True)
        var = jnp.mean(jnp.square(o - mu), axis=-1, keepdims=True)
        on = (o - mu) * lax.rsqrt(var + GN_EPS)
        gf = g_ref[:, sl].astype(F32)
        o_ref[:, sl] = (gf * jax.nn.sigmoid(gf) * on).astype(o_ref.dtype)


def _rms(v, g):
    return v * lax.rsqrt(jnp.mean(v * v, axis=-1, keepdims=True) + NORM_EPS) * g


def _outproj_kernel(oa_ref, or_ref, w_ref, x_ref, gpost_ref, gpre_ref, x1_ref, h2_ref):
    wa = oa_ref.shape[1]
    mix = _dot(oa_ref[...], w_ref[:wa, :]) + _dot(or_ref[...], w_ref[wa:, :])
    x1 = x_ref[...] + _rms(mix, gpost_ref[...])
    x1_ref[...] = x1
    h2_ref[...] = _rms(x1, gpre_ref[...]).astype(h2_ref.dtype)


def _out_proj(oa, orr, w_out, x2, g_post, g_pre, tm=512):
    T, D = x2.shape
    tm = min(tm, T)
    row = lambda w: pl.BlockSpec((tm, w), lambda i: (i, 0))
    full = lambda a: pl.BlockSpec(a.shape, lambda i: (0,) * a.ndim)
    return pl.pallas_call(
        _outproj_kernel,
        grid=(T // tm,),
        in_specs=[row(oa.shape[1]), row(orr.shape[1]), full(w_out), row(D), full(g_post), full(g_pre)],
        out_specs=[row(D), row(D)],
        out_shape=[jax.ShapeDtypeStruct((T, D), F32), jax.ShapeDtypeStruct((T, D), BF16)],
        compiler_params=pltpu.CompilerParams(
            dimension_semantics=("parallel",), vmem_limit_bytes=VMEM_LIMIT),
        name="out_proj",
    )(oa, orr, w_out, x2, g_post, g_pre)


def _mlp_kernel(h_ref, wu_ref, wd_ref, x1_ref, g_ref, o_ref):
    f = pl.program_id(1)

    @pl.when(f == 0)
    def _():
        o_ref[...] = jnp.zeros(o_ref.shape, F32)

    u = jnp.maximum(_dot(h_ref[...], wu_ref[...]), 0.0)
    o_ref[...] += _dot((u * u).astype(BF16), wd_ref[...])

    @pl.when(f == pl.num_programs(1) - 1)
    def _():
        o_ref[...] = x1_ref[...] + _rms(o_ref[...], g_ref[...])


def _mlp(h2, w_up, w_down, x1, g_post, tm=1024, tf=512):
    T, D = x1.shape
    F = w_up.shape[1]
    tm = min(tm, T)
    return pl.pallas_call(
        _mlp_kernel,
        grid=(T // tm, F // tf),
        in_specs=[
            pl.BlockSpec((tm, D), lambda i, f: (i, 0)),
            pl.BlockSpec((D, tf), lambda i, f: (0, f)),
            pl.BlockSpec((tf, D), lambda i, f: (f, 0)),
            pl.BlockSpec((tm, D), lambda i, f: (i, 0), pipeline_mode=pl.Buffered(1)),
            pl.BlockSpec((1, D), lambda i, f: (0, 0)),
        ],
        out_specs=pl.BlockSpec((tm, D), lambda i, f: (i, 0)),
        out_shape=jax.ShapeDtypeStruct((T, D), F32),
        compiler_params=pltpu.CompilerParams(
            dimension_semantics=("parallel", "arbitrary"), vmem_limit_bytes=VMEM_LIMIT),
        name="mlp",
    )(h2, w_up, w_down, x1, g_post)


def _t5_bucket_np(rel):
    n = np.maximum(rel, 0)
    max_exact = T5_BUCKETS // 2
    nf = np.maximum(n, 1).astype(np.float64)
    large = max_exact + (np.log(nf / max_exact) / np.log(T5_MAX_DIST / max_exact)
                         * (T5_BUCKETS - max_exact)).astype(np.int64)
    large = np.minimum(large, T5_BUCKETS - 1)
    return np.where(n < max_exact, n, large)


@functools.lru_cache(maxsize=None)
def _static_tables(S):
    masked = T5_BUCKETS
    i = np.arange(QB)[None, :]

    def bucket_rows(rel, valid):
        return np.where(valid, _t5_bucket_np(rel), masked)

    kk = np.arange(256)[:, None]
    rel = i - kk + QB
    b_near = bucket_rows(rel, rel >= 0)
    kk = np.arange(640)[:, None]
    rel = i - kk + WINDOW
    b_win = bucket_rows(rel, (rel >= 0) & (rel < WINDOW))
    m = np.arange(128)[:, None]
    rel = i - CMP_STRIDE * (m - CMP_PAD) - (CMP_BLOCK - 1)
    b_cmp = bucket_rows(rel, rel >= 0)
    b_const = np.full((128, QB), T5_BUCKETS - 1)
    bucket_idx = np.concatenate([b_near, b_win, b_cmp, b_const], axis=0).astype(np.int32)
    n_cmp = (S - CMP_BLOCK) // CMP_STRIDE + 1
    n_sel = S // SEL_BLOCK
    cs = np.arange(n_cmp) * CMP_STRIDE
    ss = np.arange(n_sel) * SEL_BLOCK
    overlap = (cs[:, None] <= ss[None, :] + SEL_BLOCK - 1) & (cs[:, None] + CMP_BLOCK - 1 >= ss[None, :])
    rows = CMP_PAD + S // CMP_STRIDE + 8
    msel = np.zeros((rows, 128), np.float32)
    msel[CMP_PAD:CMP_PAD + n_cmp, :n_sel] = overlap
    e_mat = (np.arange(S)[:, None] // SEL_BLOCK == np.arange(128)[None, :]).astype(np.float32)
    log_gamma = np.log(1.0 - np.exp2(-5.0 - np.arange(RET_HEADS, dtype=np.float32))).astype(np.float32)
    idx = np.arange(RET_CHUNK, dtype=np.float32)
    diff = idx[:, None] - idx[None, :]
    decay = np.where(diff[None] >= 0, np.exp(np.maximum(diff, 0.0)[None] * log_gamma[:, None, None]), 0.0)
    xi = np.exp((idx + 1.0)[None, :] * log_gamma[:, None])
    zeta = np.exp((RET_CHUNK - 1.0 - idx)[None, :] * log_gamma[:, None])
    xz = np.stack([np.broadcast_to(xi[:, :, None], (RET_HEADS, RET_CHUNK, 128)),
                   np.broadcast_to(zeta[:, :, None], (RET_HEADS, RET_CHUNK, 128))], axis=1)
    g_c = np.exp(RET_CHUNK * log_gamma)
    inv_freq = ROPE_BASE ** (-np.arange(0, RET_D, 2, dtype=np.float32) / RET_D)
    ang = np.arange(S, dtype=np.float32)[:, None] * inv_freq[None, :]
    return dict(bucket_idx=bucket_idx, msel=msel, e_mat=e_mat, decay=decay.astype(np.float32),
                xz=xz.astype(np.float32), g_c=g_c.astype(np.float32), ang=ang.astype(np.float32))


def _bias_table(t5_bias, st):
    assert st["bucket_idx"].shape == (TAB_ROWS, QB)
    idx = jnp.asarray(st["bucket_idx"].reshape(-1))
    onehot = (idx[None, :] == jnp.arange(T5_BUCKETS + 1, dtype=jnp.int32)[:, None]).astype(F32)
    vals = jnp.concatenate([t5_bias.astype(F32) * LOG2E, jnp.full((1, NSA_HEADS), NEG, F32)], axis=0)
    tab = jnp.einsum("bh,bn->hn", vals, onehot, precision=lax.Precision.HIGHEST)
    tab = tab.reshape(NSA_G, NSA_HG, TAB_ROWS, QB).transpose(0, 2, 1, 3)
    return tab.reshape(NSA_G, TAB_ROWS, NSA_HG * QB)


def kernel(x, norm_mix_pre, w_in, cmp_pe_k, cmp_w1_k, cmp_b1_k, cmp_w2_k, cmp_pe_v, cmp_w1_v, cmp_b1_v,
           cmp_w2_v, t5_bias, w_out, norm_mix_post, norm_mlp_pre, w_up, w_down, norm_mlp_post):
    B, S, D = x.shape
    T = B * S
    depth = w_in.shape[0]
    st = _static_tables(S)
    tab = _bias_table(t5_bias, st)
    msel = jnp.asarray(st["msel"])
    e_mat = jnp.asarray(st["e_mat"], BF16)
    decay = jnp.asarray(st["decay"])
    xz = jnp.asarray(st["xz"])
    gc = jnp.asarray(st["g_c"])
    ang = jnp.asarray(st["ang"])
    cos, sin = jnp.cos(ang), jnp.sin(ang)

    colscale = np.ones((1, N_PROJ), np.float32)
    colscale[0, COL_QA:COL_QA + NSA_HEADS * DK] = DK ** -0.5 * LOG2E
    colscale[0, COL_KR:COL_KR + RET_HEADS * RET_D] = RET_D ** -0.5
    colscale = jnp.asarray(colscale)

    xcur = x.reshape(T, D)
    for l in range(depth):
        w_re = _weight_prep(jnp.transpose(w_in[l]))
        proj = _in_proj(xcur, norm_mix_pre[l][None, :], w_re, colscale)

        xc, vt = _relayout(proj, B, S)
        half = CMP_STRIDE * DK
        pe = jnp.stack([cmp_pe_k[l].reshape(2, half), cmp_pe_v[l].reshape(2, half)])
        pe = jnp.concatenate([pe, jnp.zeros((2, 6, half), F32)], axis=1)
        w1 = jnp.stack([cmp_w1_k[l], cmp_w1_v[l]]).astype(BF16)
        b1 = jnp.stack([cmp_b1_k[l], cmp_b1_v[l]])[:, None, :]
        w2 = jnp.stack([cmp_w2_k[l], cmp_w2_v[l]]).astype(BF16)
        kvc = _compress(xc, pe, w1, b1, w2)

        o_a, o_r = _token_mix(proj, vt, kvc, msel, e_mat, tab, cos, sin, decay, xz, gc, B, S)

        x1, h2 = _out_proj(o_a, o_r, w_out[l].astype(BF16), xcur,
                           norm_mix_post[l][None, :], norm_mlp_pre[l][None, :])
        xcur = _mlp(h2, w_up[l].astype(BF16), w_down[l].astype(BF16), x1, norm_mlp_post[l][None, :])
    return xcur.reshape(B, S, D)
```

```python
import functools

import numpy as np
import jax
import jax.numpy as jnp
from jax import lax
from jax.experimental import pallas as pl
from jax.experimental.pallas import tpu as pltpu

F32 = jnp.float32
BF16 = jnp.bfloat16

NSA_HEADS = 8
NSA_G = 2
NSA_HG = 4
DK = 128
CMP_BLOCK = 32
CMP_STRIDE = 16
SEL_BLOCK = 64
SEL_TOPN = 16
WINDOW = 512
QB = 128
RET_HEADS = 4
RET_D = 256
RET_CHUNK = 128
ROPE_BASE = 10000.0
GN_EPS = 1e-6
NORM_EPS = 1e-6
T5_BUCKETS = 32
T5_MAX_DIST = 128
FORCE = 1e4
NEG = -1e30
M_FLOOR = -1e20
LOG2E = 1.4426950408889634

COL_QA = 0
COL_QR = 1024
COL_KR = 2048
COL_VR = 3072
COL_GR = 4096
COL_KV = 5120
COL_GATE = 6656
N_PROJ = 6912
W_KV = NSA_HEADS * DK
W_GATE = W_KV + 6 * NSA_G * DK
W_RET = W_GATE + 3 * NSA_HEADS
CMP_PAD = 120

VMEM_LIMIT = 56 * 1024 * 1024


def _dot(a, b):
    return jnp.dot(a, b, preferred_element_type=F32)


def _dot_nt(a, b):
    return lax.dot_general(a, b, (((1,), (1,)), ((), ())), preferred_element_type=F32)


def _inproj_kernel(x_ref, g_ref, w_ref, cs_ref, o_ref, h_scr):
    @pl.when(pl.program_id(1) == 0)
    def _():
        xf = x_ref[...]
        ms = jnp.mean(xf * xf, axis=-1, keepdims=True)
        h_scr[...] = (xf * lax.rsqrt(ms + NORM_EPS) * g_ref[...]).astype(BF16)

    acc = _dot_nt(h_scr[...], w_ref[...])
    o_ref[...] = (acc * cs_ref[...]).astype(o_ref.dtype)


PREP_ROWS = 256
PREP_T_RET = COL_QR // PREP_ROWS
PREP_T_KV = COL_KV // PREP_ROWS
PREP_T_GATE = COL_GATE // PREP_ROWS
PREP_SHIFT = W_RET % PREP_ROWS


def _weight_prep_kernel(a_ref, b_ref, o_ref):
    i = pl.program_id(0)
    aligned = (i < PREP_T_RET) | ((i >= PREP_T_KV) & (i < PREP_T_GATE))

    @pl.when(aligned)
    def _():
        o_ref[...] = a_ref[...].astype(BF16)

    @pl.when((i >= PREP_T_RET) & (i < PREP_T_KV))
    def _():
        o_ref[...] = jnp.concatenate([a_ref[PREP_SHIFT:, :], b_ref[:PREP_SHIFT, :]], axis=0).astype(BF16)

    @pl.when(i == PREP_T_GATE)
    def _():
        ng = 3 * NSA_HG
        a = a_ref[:128, :]
        row = lax.broadcasted_iota(jnp.int32, a.shape, 0)
        g0 = jnp.where(row < ng, a, 0.0)
        g1 = jnp.where(row < ng, pltpu.roll(a, 128 - ng, 0), 0.0)
        o_ref[...] = jnp.concatenate([g0, g1], axis=0).astype(BF16)


def _weight_prep(w_t):
    n_in, D = w_t.shape
    assert W_KV % PREP_ROWS == 0 and W_GATE % PREP_ROWS == 0 and N_PROJ - COL_GATE == PREP_ROWS
    src_ret = W_RET // PREP_ROWS - PREP_T_RET
    src_kv = W_KV // PREP_ROWS - PREP_T_KV

    def a_map(i):
        t = jnp.where(i < PREP_T_RET, i,
                      jnp.where(i < PREP_T_KV, i + src_ret,
                                jnp.where(i < PREP_T_GATE, i + src_kv, W_GATE // PREP_ROWS)))
        return (t, 0)

    b_rows = 32
    assert PREP_SHIFT <= b_rows and PREP_ROWS % b_rows == 0

    def b_map(i):
        nxt = (i + src_ret + 1) * (PREP_ROWS // b_rows)
        return (jnp.where((i >= PREP_T_RET) & (i < PREP_T_KV), nxt, 0), 0)

    return pl.pallas_call(
        _weight_prep_kernel,
        grid=(N_PROJ // PREP_ROWS,),
        in_specs=[pl.BlockSpec((PREP_ROWS, D), a_map), pl.BlockSpec((b_rows, D), b_map)],
        out_specs=pl.BlockSpec((PREP_ROWS, D), lambda i: (i, 0)),
        out_shape=jax.ShapeDtypeStruct((N_PROJ, D), BF16),
        compiler_params=pltpu.CompilerParams(
            dimension_semantics=("arbitrary",), vmem_limit_bytes=VMEM_LIMIT),
        name="weight_prep",
    )(w_t, w_t)


def _in_proj(x2, gain, w, colscale, tm=1024, tn=768):
    T, D = x2.shape
    N = w.shape[0]
    tm = min(tm, T)
    return pl.pallas_call(
        _inproj_kernel,
        grid=(T // tm, N // tn),
        in_specs=[
            pl.BlockSpec((tm, D), lambda i, j: (i, 0)),
            pl.BlockSpec((1, D), lambda i, j: (0, 0)),
            pl.BlockSpec((tn, D), lambda i, j: (j, 0)),
            pl.BlockSpec((1, tn), lambda i, j: (0, j)),
        ],
        out_specs=pl.BlockSpec((tm, tn), lambda i, j: (i, j)),
        out_shape=jax.ShapeDtypeStruct((T, N), BF16),
        scratch_shapes=[pltpu.VMEM((tm, D), BF16)],
        compiler_params=pltpu.CompilerParams(
            dimension_semantics=("parallel", "arbitrary"), vmem_limit_bytes=VMEM_LIMIT),
        name="in_proj",
    )(x2, gain, w, colscale)


def _relayout_kernel(cmp_ref, vs_ref, vw_ref, xc_ref, vt_ref, t_scr):
    tm = cmp_ref.shape[0]
    for s in range(4):
        t_scr[s] = cmp_ref[:, s * DK:(s + 1) * DK].astype(F32)
        for t in range(CMP_STRIDE):
            xc_ref[0, s, :, t * DK:(t + 1) * DK] = t_scr[
                s, pl.ds(t, tm // CMP_STRIDE, stride=CMP_STRIDE), :].astype(BF16)
    for kind, ref in enumerate((vs_ref, vw_ref)):
        v = ref[...].astype(F32)
        for g in range(NSA_G):
            for r in range(tm // QB):
                vt_ref[kind, 0, g, r] = v[r * QB:(r + 1) * QB, g * DK:(g + 1) * DK].T.astype(BF16)


def _relayout(proj, B, S, tm=1024):
    tm = min(tm, S)
    rpb = S // tm
    NC, NQ = S // CMP_STRIDE, S // QB
    return pl.pallas_call(
        _relayout_kernel,
        grid=(B * rpb,),
        in_specs=[
            pl.BlockSpec((tm, 4 * DK), lambda i: (i, COL_KV // (4 * DK))),
            pl.BlockSpec((tm, 2 * DK), lambda i: (i, (COL_KV + 6 * DK) // (2 * DK))),
            pl.BlockSpec((tm, 2 * DK), lambda i: (i, (COL_KV + 10 * DK) // (2 * DK))),
        ],
        out_specs=[
            pl.BlockSpec((1, 4, tm // CMP_STRIDE, CMP_STRIDE * DK), lambda i: (i // rpb, 0, i % rpb, 0)),
            pl.BlockSpec((2, 1, NSA_G, tm // QB, DK, QB), lambda i: (0, i // rpb, 0, i % rpb, 0, 0)),
        ],
        out_shape=[jax.ShapeDtypeStruct((B, 4, NC, CMP_STRIDE * DK), BF16),
                   jax.ShapeDtypeStruct((2, B, NSA_G, NQ, DK, QB), BF16)],
        scratch_shapes=[pltpu.VMEM((4, tm, DK), F32)],
        compiler_params=pltpu.CompilerParams(
            dimension_semantics=("parallel",), vmem_limit_bytes=VMEM_LIMIT),
        name="kv_relayout",
    )(proj, proj, proj)


def _compress_kernel(x_ref, pe_ref, w1_ref, b1_ref, w2_ref, o_ref):
    nc = x_ref.shape[2]
    half = x_ref.shape[3]
    xf = x_ref[0, 0].astype(F32)
    xa = (xf + pe_ref[0, 0:1, :]).astype(BF16)
    xb = (xf + pe_ref[0, 1:2, :]).astype(BF16)
    a = _dot(xa, w1_ref[0, :half, :])
    b = _dot(xb, w1_ref[0, half:, :])
    b_next = pltpu.roll(b, nc - 1, 0)
    hid = jax.nn.gelu(a + b_next + b1_ref[0])
    out = _dot(hid.astype(BF16), w2_ref[0])
    o_ref[0, 0, :CMP_PAD, :] = jnp.zeros((CMP_PAD, DK), F32)
    o_ref[0, 0, CMP_PAD:CMP_PAD + nc, :] = out
    o_ref[0, 0, CMP_PAD + nc:, :] = jnp.zeros((o_ref.shape[2] - CMP_PAD - nc, DK), F32)


def _compress(xc, pe, w1, b1, w2):
    B, _, NC, half = xc.shape
    rows = CMP_PAD + NC + 8
    return pl.pallas_call(
        _compress_kernel,
        grid=(B, 4),
        in_specs=[
            pl.BlockSpec((1, 1, NC, half), lambda b, s: (b, s, 0, 0)),
            pl.BlockSpec((1, 8, half), lambda b, s: (s // 2, 0, 0)),
            pl.BlockSpec((1, 2 * half, DK), lambda b, s: (s // 2, 0, 0)),
            pl.BlockSpec((1, 1, DK), lambda b, s: (s // 2, 0, 0)),
            pl.BlockSpec((1, DK, DK), lambda b, s: (s // 2, 0, 0)),
        ],
        out_specs=pl.BlockSpec((1, 1, rows, DK), lambda b, s: (b, s, 0, 0)),
        out_shape=jax.ShapeDtypeStruct((B, 4, rows, DK), F32),
        compiler_params=pltpu.CompilerParams(
            dimension_semantics=("parallel", "parallel"), vmem_limit_bytes=VMEM_LIMIT),
        name="compress_kv",
    )(xc, pe, w1, b1, w2)


TAB_NEAR, TAB_WIN, TAB_CMP, TAB_CONST, TAB_ROWS = 0, 256, 896, 1024, 1152
ONES_ROWS = 16
SEL_UNROLL = 4


def _token_mix_kernel(q_ref, ksl_ref, vslt_ref, kw_ref, vwt_ref, gate_ref, kc_ref, vc_ref, msel_ref, e_ref, tab_ref,
                      rq_ref, rk_ref, rv_ref, rg_ref, cos_ref, sin_ref, dec_ref, xz_ref, gc_ref,
                      o_ref, or_ref, acc_ref, m_ref, l_ref, out_ref, p_ref, a_ref, r_scr):
    qb = pl.program_id(1)
    groups = range(NSA_G)

    @pl.when(qb == 0)
    def _():
        r_scr[...] = jnp.zeros(r_scr.shape, F32)

    retention = functools.partial(
        _retention_heads, q_ref=rq_ref, k_ref=rk_ref, v_ref=rv_ref, g_ref=rg_ref, cos_ref=cos_ref, sin_ref=sin_ref,
        dec_ref=dec_ref, xz_ref=xz_ref, gc_ref=gc_ref, o_ref=or_ref, r_scr=r_scr)
    cols = NSA_HG * QB
    sub = lax.broadcasted_iota(jnp.int32, (128, cols), 0)
    qs_t, g_t = [], []
    for g in groups:
        qf = q_ref[:, g * cols:(g + 1) * cols].astype(F32)
        qs_t.append(jnp.concatenate([qf[:, h * DK:(h + 1) * DK].T for h in range(NSA_HG)],
                                    axis=1).astype(BF16))
        g_t.append(jax.nn.sigmoid(gate_ref[:, g * 128:(g + 1) * 128].astype(F32)).T)

    def gate_row(g, c):
        return jnp.concatenate([g_t[g][3 * h + c:3 * h + c + 1, :] for h in range(NSA_HG)], axis=1)

    retention(range(0, RET_HEADS // 2))

    for g in groups:
        m_ref[g] = jnp.full((1, cols), M_FLOOR, F32)
        l_ref[g] = jnp.zeros((1, cols), F32)
        acc_ref[g] = jnp.zeros((256, cols), F32)

    def cmp_chunk(g, start, bias):
        kc = kc_ref[0, g, pl.ds(start, 128), :].astype(BF16)
        vc_t = vc_ref[0, g, pl.ds(start, 128), :].T
        ms_t = msel_ref[pl.ds(start, 128), :].T
        s = _dot(kc, qs_t[g]) + bias
        m_old = m_ref[g]
        m_new = jnp.maximum(m_old, jnp.max(s, axis=0, keepdims=True))
        alpha = jnp.exp2(m_old - m_new)
        p = jnp.exp2(s - m_new)
        l_ref[g] = alpha * l_ref[g] + jnp.sum(p, axis=0, keepdims=True)
        lhs = jnp.concatenate([vc_t, ms_t], axis=0).astype(BF16)
        acc_ref[g] = alpha * acc_ref[g] + _dot(lhs, p.astype(BF16))
        m_ref[g] = m_new

    near_start = pl.multiple_of(8 * qb, 8)
    near_mask = jnp.where(sub + (8 * qb - CMP_PAD) < 0, NEG, 0.0)
    for g in groups:
        cmp_chunk(g, near_start, tab_ref[g, TAB_CMP:TAB_CMP + 128, :] + near_mask)

    def far_body(c, carry):
        st = pl.multiple_of(CMP_PAD + 128 * c, 8)
        far_mask = jnp.where(128 * c + sub < 8 * qb - CMP_PAD, 0.0, NEG)
        for g in groups:
            cmp_chunk(g, st, tab_ref[g, TAB_CONST:TAB_CONST + 128, :] + far_mask)
        return carry

    lax.fori_loop(0, (8 * qb + 7) // 128, far_body, 0)

    imp_t = []
    for g in groups:
        l = l_ref[g]
        inv = jnp.where(l > 0.0, 1.0 / l, 0.0)
        accv = acc_ref[g]
        out_ref[g] = accv[:DK] * (gate_row(g, 0) * inv)
        u = accv[DK:] * inv
        imp_t.append(u[:, 0:QB] + u[:, QB:2 * QB] + u[:, 2 * QB:3 * QB] + u[:, 3 * QB:4 * QB])

    tiles = [jnp.maximum(qb + d, 0) for d in range(-4, 1)]
    win_ones = jnp.ones((ONES_ROWS, 5 * QB), BF16)
    for g in groups:
        kwin = jnp.concatenate([kw_ref[pl.ds(pl.multiple_of(t * QB, QB), QB), g * DK:(g + 1) * DK]
                                for t in tiles], axis=0)
        vwin_t = jnp.concatenate([vwt_ref[0, 0, g, t] for t in tiles], axis=1)
        lhs = jnp.concatenate([vwin_t, win_ones], axis=0)
        bias = jnp.concatenate(
            [tab_ref[g, TAB_WIN + i * QB:TAB_WIN + (i + 1) * QB, :] + jnp.where(qb + d < 0, NEG, 0.0)
             for i, d in enumerate(range(-4, 1))], axis=0)
        s = _dot(kwin, qs_t[g]) + bias
        p = jnp.exp2(s - jnp.max(s, axis=0, keepdims=True))
        r = _dot(lhs, p.astype(BF16))
        out_ref[g] += gate_row(g, 2) * (r[:DK] / r[DK:DK + 1])

    ji = lax.broadcasted_iota(jnp.int32, (128, NSA_G * QB), 0)
    qi = lax.broadcasted_iota(jnp.int32, (128, NSA_G * QB), 1) & (QB - 1)
    cur = 2 * qb + (qi >= SEL_BLOCK).astype(jnp.int32)
    valid = ji <= cur
    forced = (ji == 0) | (ji == cur) | (ji == cur - 1)
    taken = -jnp.inf
    vt = jnp.where(forced, taken, jnp.where(valid, jnp.concatenate(imp_t, axis=1), -1.0))
    jio = ji.astype(F32)
    for _ in range(SEL_TOPN - 3):
        mx = jnp.max(vt, axis=0, keepdims=True)
        idx = jnp.min(jnp.where(vt == mx, jio, 128.0), axis=0, keepdims=True)
        vt = jnp.where(jio == idx, taken, vt)
    sb_all = jnp.where((vt == taken) & valid, 0.0, NEG)
    sb_far_all = jnp.where(ji < 2 * qb - 2, sb_all, NEG)
    q_far, q_near = [], []
    for g in groups:
        sb_t = sb_all[:, g * QB:(g + 1) * QB]
        sb_far = sb_far_all[:, g * QB:(g + 1) * QB]
        aug_far = (jnp.concatenate([sb_far] * NSA_HG, axis=1) + tab_ref[g, TAB_CONST:TAB_CONST + 128, :]).astype(BF16)
        aug_near = jnp.concatenate([sb_t] * NSA_HG, axis=1).astype(BF16)
        q_far.append(jnp.concatenate([qs_t[g], aug_far], axis=0))
        q_near.append(jnp.concatenate([qs_t[g], aug_near], axis=0))

    ones_t = jnp.ones((ONES_ROWS, 256), BF16)
    n_tiles = e_ref.shape[0] // 256
    arows = DK + ONES_ROWS

    def k_aug(g, t):
        st = pl.multiple_of(t * QB, QB)
        return jnp.concatenate([ksl_ref[pl.ds(st, QB), g * DK:(g + 1) * DK], e_ref[pl.ds(st, QB), :]], axis=1)

    def scores(g, t):
        tt = jnp.clip(t, 0, n_tiles - 1)
        return _dot(jnp.concatenate([k_aug(g, 2 * tt), k_aug(g, 2 * tt + 1)], axis=0), q_far[g])

    def softmax_stage(g, s, slot):
        m_old = m_ref[g]
        m_new = jnp.maximum(m_old, jnp.max(s, axis=0, keepdims=True))
        a_ref[g, slot] = jnp.exp2(m_old - m_new)
        p_ref[g, slot] = jnp.exp2(s - m_new).astype(BF16)
        m_ref[g] = m_new

    def values_stage(g, v_t, slot):
        lhs = jnp.concatenate([v_t, ones_t], axis=0)
        acc_ref[g, :arows] = a_ref[g, slot] * acc_ref[g, :arows] + _dot(lhs, p_ref[g, slot])

    def v_far(g, t):
        tt = jnp.clip(t, 0, n_tiles - 1)
        return jnp.concatenate([vslt_ref[0, 0, g, 2 * tt], vslt_ref[0, 0, g, 2 * tt + 1]], axis=1)

    n_far = qb // 2
    for g in groups:
        m_ref[g] = jnp.full((1, cols), M_FLOOR, F32)
        acc_ref[g] = jnp.zeros((256, cols), F32)
        softmax_stage(g, scores(g, 0), 0)

    def sel_far_body(v, carry):
        a = SEL_UNROLL * v
        for k in range(SEL_UNROLL):
            cur_slot, other = k % 2, 1 - k % 2
            for g in groups:
                softmax_stage(g, scores(g, a + k + 1), other)
                values_stage(g, v_far(g, a + k), cur_slot)
        return carry

    n_iter = (n_far + SEL_UNROLL - 1) // SEL_UNROLL
    lax.fori_loop(0, n_iter, sel_far_body, 0)

    retention(range(RET_HEADS // 2, RET_HEADS))
    tp = jnp.maximum(qb - 1, 0)
    first_mask = jnp.where(qb == 0, NEG, 0.0)
    for g in groups:
        ka = jnp.concatenate([k_aug(g, tp), k_aug(g, qb)], axis=0)
        bias = jnp.concatenate([tab_ref[g, TAB_NEAR:TAB_NEAR + QB, :] + first_mask,
                                tab_ref[g, TAB_NEAR + QB:TAB_NEAR + 2 * QB, :]], axis=0)
        softmax_stage(g, _dot(ka, q_near[g]) + bias, 0)
        values_stage(g, jnp.concatenate([vslt_ref[0, 0, g, tp], vslt_ref[0, 0, g, qb]], axis=1), 0)
        accv = acc_ref[g, :arows]
        res = out_ref[g] + gate_row(g, 1) * (accv[:DK] / accv[DK:DK + 1])
        for h in range(NSA_HG):
            c0 = (g * NSA_HG + h) * DK
            o_ref[:, c0:c0 + DK] = res[:, h * QB:(h + 1) * QB].T.astype(o_ref.dtype)


def _token_mix(proj, vt, kvc, msel, e_mat, tab, cos, sin, decay, xz, gc, B, S):
    T = B * S
    NQ = S // QB
    cols = NSA_HG * QB
    rows_c = kvc.shape[2]
    once = pl.Buffered(1)
    seq = lambda c0: pl.BlockSpec((S, NSA_G * DK), lambda b, i: (b, c0 // (NSA_G * DK)), pipeline_mode=once)
    vt_spec = lambda kind: pl.BlockSpec((1, 1, NSA_G, NQ, DK, QB), lambda b, i: (kind, b, 0, 0, 0, 0),
                                        pipeline_mode=once)
    const = lambda a: pl.BlockSpec(a.shape, lambda b, i: (0,) * a.ndim, pipeline_mode=once)
    ret_w = RET_HEADS * RET_D
    assert RET_CHUNK == QB
    ret_sec = lambda c0: pl.BlockSpec((RET_CHUNK, ret_w), lambda b, i: (b * NQ + i, c0 // ret_w))
    return pl.pallas_call(
        _token_mix_kernel,
        grid=(B, NQ),
        in_specs=[
            pl.BlockSpec((QB, NSA_HEADS * DK), lambda b, i: (b * NQ + i, 0)),
            seq(COL_KV + 4 * DK), vt_spec(0), seq(COL_KV + 8 * DK), vt_spec(1),
            pl.BlockSpec((QB, NSA_G * 128), lambda b, i: (b * NQ + i, COL_GATE // (NSA_G * 128))),
            pl.BlockSpec((1, NSA_G, rows_c, DK), lambda b, i: (b, 0, 0, 0)),
            pl.BlockSpec((1, NSA_G, rows_c, DK), lambda b, i: (b, 1, 0, 0)),
            const(msel), const(e_mat), const(tab),
            ret_sec(COL_QR), ret_sec(COL_KR), ret_sec(COL_VR), ret_sec(COL_GR),
            pl.BlockSpec((RET_CHUNK, RET_D // 2), lambda b, i: (i, 0)),
            pl.BlockSpec((RET_CHUNK, RET_D // 2), lambda b, i: (i, 0)),
            const(decay), const(xz),
            pl.BlockSpec(memory_space=pltpu.SMEM),
        ],
        out_specs=[pl.BlockSpec((QB, NSA_HEADS * DK), lambda b, i: (b * NQ + i, 0)),
                   pl.BlockSpec((RET_CHUNK, ret_w), lambda b, i: (b * NQ + i, 0))],
        out_shape=[jax.ShapeDtypeStruct((T, NSA_HEADS * DK), BF16),
                   jax.ShapeDtypeStruct((T, ret_w), BF16)],
        scratch_shapes=[
            pltpu.VMEM((NSA_G, 256, cols), F32),
            pltpu.VMEM((NSA_G, 1, cols), F32),
            pltpu.VMEM((NSA_G, 1, cols), F32),
            pltpu.VMEM((NSA_G, DK, cols), F32),
            pltpu.VMEM((NSA_G, 2, 256, cols), BF16),
            pltpu.VMEM((NSA_G, 2, 1, cols), F32),
            pltpu.VMEM((RET_HEADS, RET_D, RET_D), F32),
        ],
        compiler_params=pltpu.CompilerParams(
            dimension_semantics=("parallel", "arbitrary"), vmem_limit_bytes=VMEM_LIMIT),
        name="token_mix",
    )(proj, proj, vt, proj, vt, proj, kvc, kvc, msel, e_mat, tab, proj, proj, proj, proj, cos, sin, decay, xz, gc)


def _retention_heads(heads, q_ref, k_ref, v_ref, g_ref, cos_ref, sin_ref, dec_ref, xz_ref, gc_ref, o_ref, r_scr):
    cos = cos_ref[...]
    sin = sin_ref[...]
    hd = RET_D // 2

    def rot(x):
        x1, x2 = x[:, :hd], x[:, hd:]
        return jnp.concatenate([x1 * cos - x2 * sin, x2 * cos + x1 * sin], axis=1)

    for h in heads:
        sl = slice(h * RET_D, (h + 1) * RET_D)
        qh = rot(q_ref[:, sl].astype(F32))
        kh = rot(k_ref[:, sl].astype(F32))
        vh = v_ref[:, sl]
        xi = xz_ref[h, 0]
        zeta = xz_ref[h, 1]
        qb16 = qh.astype(BF16)
        inner = _dot_nt(qb16, kh.astype(BF16)) * dec_ref[h]
        o = _dot(inner.astype(BF16), vh)
        r_old = r_scr[h]
        cross = _dot(qb16, r_old.astype(BF16))
        o = o + cross * jnp.concatenate([xi, xi], axis=1)
        kz = (kh * jnp.concatenate([zeta, zeta], axis=1)).astype(BF16)
        r_scr[h] = r_old * gc_ref[h] + _dot(kz.T, vh)
        mu = jnp.mean(o, axis=-1, keepdims=True)
        var = jnp.mean(jnp.square(o - mu), axis=-1, keepdims=True)
        on = (o - mu) * lax.rsqrt(var + GN_EPS)
        gf = g_ref[:, sl].astype(F32)
        o_ref[:, sl] = (gf * jax.nn.sigmoid(gf) * on).astype(o_ref.dtype)


def _rms(v, g):
    return v * lax.rsqrt(jnp.mean(v * v, axis=-1, keepdims=True) + NORM_EPS) * g


def _outproj_kernel(oa_ref, or_ref, w_ref, x_ref, gpost_ref, gpre_ref, x1_ref, h2_ref):
    wa = oa_ref.shape[1]
    mix = _dot(oa_ref[...], w_ref[:wa, :]) + _dot(or_ref[...], w_ref[wa:, :])
    x1 = x_ref[...] + _rms(mix, gpost_ref[...])
    x1_ref[...] = x1
    h2_ref[...] = _rms(x1, gpre_ref[...]).astype(h2_ref.dtype)


def _out_proj(oa, orr, w_out, x2, g_post, g_pre, tm=512):
    T, D = x2.shape
    tm = min(tm, T)
    row = lambda w: pl.BlockSpec((tm, w), lambda i: (i, 0))
    full = lambda a: pl.BlockSpec(a.shape, lambda i: (0,) * a.ndim)
    return pl.pallas_call(
        _outproj_kernel,
        grid=(T // tm,),
        in_specs=[row(oa.shape[1]), row(orr.shape[1]), full(w_out), row(D), full(g_post), full(g_pre)],
        out_specs=[row(D), row(D)],
        out_shape=[jax.ShapeDtypeStruct((T, D), F32), jax.ShapeDtypeStruct((T, D), BF16)],
        compiler_params=pltpu.CompilerParams(
            dimension_semantics=("parallel",), vmem_limit_bytes=VMEM_LIMIT),
        name="out_proj",
    )(oa, orr, w_out, x2, g_post, g_pre)


def _mlp_kernel(h_ref, wu_ref, wd_ref, x1_ref, g_ref, o_ref):
    f = pl.program_id(1)

    @pl.when(f == 0)
    def _():
        o_ref[...] = jnp.zeros(o_ref.shape, F32)

    u = jnp.maximum(_dot(h_ref[...], wu_ref[...]), 0.0)
    o_ref[...] += _dot((u * u).astype(BF16), wd_ref[...])

    @pl.when(f == pl.num_programs(1) - 1)
    def _():
        o_ref[...] = x1_ref[...] + _rms(o_ref[...], g_ref[...])


def _mlp(h2, w_up, w_down, x1, g_post, tm=1024, tf=512):
    T, D = x1.shape
    F = w_up.shape[1]
    tm = min(tm, T)
    return pl.pallas_call(
        _mlp_kernel,
        grid=(T // tm, F // tf),
        in_specs=[
            pl.BlockSpec((tm, D), lambda i, f: (i, 0)),
            pl.BlockSpec((D, tf), lambda i, f: (0, f)),
            pl.BlockSpec((tf, D), lambda i, f: (f, 0)),
            pl.BlockSpec((tm, D), lambda i, f: (i, 0), pipeline_mode=pl.Buffered(1)),
            pl.BlockSpec((1, D), lambda i, f: (0, 0)),
        ],
        out_specs=pl.BlockSpec((tm, D), lambda i, f: (i, 0)),
        out_shape=jax.ShapeDtypeStruct((T, D), F32),
        compiler_params=pltpu.CompilerParams(
            dimension_semantics=("parallel", "arbitrary"), vmem_limit_bytes=VMEM_LIMIT),
        name="mlp",
    )(h2, w_up, w_down, x1, g_post)


def _t5_bucket_np(rel):
    n = np.maximum(rel, 0)
    max_exact = T5_BUCKETS // 2
    nf = np.maximum(n, 1).astype(np.float64)
    large = max_exact + (np.log(nf / max_exact) / np.log(T5_MAX_DIST / max_exact)
                         * (T5_BUCKETS - max_exact)).astype(np.int64)
    large = np.minimum(large, T5_BUCKETS - 1)
    return np.where(n < max_exact, n, large)


@functools.lru_cache(maxsize=None)
def _static_tables(S):
    masked = T5_BUCKETS
    i = np.arange(QB)[None, :]

    def bucket_rows(rel, valid):
        return np.where(valid, _t5_bucket_np(rel), masked)

    kk = np.arange(256)[:, None]
    rel = i - kk + QB
    b_near = bucket_rows(rel, rel >= 0)
    kk = np.arange(640)[:, None]
    rel = i - kk + WINDOW
    b_win = bucket_rows(rel, (rel >= 0) & (rel < WINDOW))
    m = np.arange(128)[:, None]
    rel = i - CMP_STRIDE * (m - CMP_PAD) - (CMP_BLOCK - 1)
    b_cmp = bucket_rows(rel, rel >= 0)
    b_const = np.full((128, QB), T5_BUCKETS - 1)
    bucket_idx = np.concatenate([b_near, b_win, b_cmp, b_const], axis=0).astype(np.int32)
    n_cmp = (S - CMP_BLOCK) // CMP_STRIDE + 1
    n_sel = S // SEL_BLOCK
    cs = np.arange(n_cmp) * CMP_STRIDE
    ss = np.arange(n_sel) * SEL_BLOCK
    overlap = (cs[:, None] <= ss[None, :] + SEL_BLOCK - 1) & (cs[:, None] + CMP_BLOCK - 1 >= ss[None, :])
    rows = CMP_PAD + S // CMP_STRIDE + 8
    msel = np.zeros((rows, 128), np.float32)
    msel[CMP_PAD:CMP_PAD + n_cmp, :n_sel] = overlap
    e_mat = (np.arange(S)[:, None] // SEL_BLOCK == np.arange(128)[None, :]).astype(np.float32)
    log_gamma = np.log(1.0 - np.exp2(-5.0 - np.arange(RET_HEADS, dtype=np.float32))).astype(np.float32)
    idx = np.arange(RET_CHUNK, dtype=np.float32)
    diff = idx[:, None] - idx[None, :]
    decay = np.where(diff[None] >= 0, np.exp(np.maximum(diff, 0.0)[None] * log_gamma[:, None, None]), 0.0)
    xi = np.exp((idx + 1.0)[None, :] * log_gamma[:, None])
    zeta = np.exp((RET_CHUNK - 1.0 - idx)[None, :] * log_gamma[:, None])
    xz = np.stack([np.broadcast_to(xi[:, :, None], (RET_HEADS, RET_CHUNK, 128)),
                   np.broadcast_to(zeta[:, :, None], (RET_HEADS, RET_CHUNK, 128))], axis=1)
    g_c = np.exp(RET_CHUNK * log_gamma)
    inv_freq = ROPE_BASE ** (-np.arange(0, RET_D, 2, dtype=np.float32) / RET_D)
    ang = np.arange(S, dtype=np.float32)[:, None] * inv_freq[None, :]
    return dict(bucket_idx=bucket_idx, msel=msel, e_mat=e_mat, decay=decay.astype(np.float32),
                xz=xz.astype(np.float32), g_c=g_c.astype(np.float32), ang=ang.astype(np.float32))


def _bias_table(t5_bias, st):
    assert st["bucket_idx"].shape == (TAB_ROWS, QB)
    idx = jnp.asarray(st["bucket_idx"].reshape(-1))
    onehot = (idx[None, :] == jnp.arange(T5_BUCKETS + 1, dtype=jnp.int32)[:, None]).astype(F32)
    vals = jnp.concatenate([t5_bias.astype(F32) * LOG2E, jnp.full((1, NSA_HEADS), NEG, F32)], axis=0)
    tab = jnp.einsum("bh,bn->hn", vals, onehot, precision=lax.Precision.HIGHEST)
    tab = tab.reshape(NSA_G, NSA_HG, TAB_ROWS, QB).transpose(0, 2, 1, 3)
    return tab.reshape(NSA_G, TAB_ROWS, NSA_HG * QB)


def kernel(x, norm_mix_pre, w_in, cmp_pe_k, cmp_w1_k, cmp_b1_k, cmp_w2_k, cmp_pe_v, cmp_w1_v, cmp_b1_v,
           cmp_w2_v, t5_bias, w_out, norm_mix_post, norm_mlp_pre, w_up, w_down, norm_mlp_post):
    B, S, D = x.shape
    T = B * S
    depth = w_in.shape[0]
    st = _static_tables(S)
    tab = _bias_table(t5_bias, st)
    msel = jnp.asarray(st["msel"])
    e_mat = jnp.asarray(st["e_mat"], BF16)
    decay = jnp.asarray(st["decay"])
    xz = jnp.asarray(st["xz"])
    gc = jnp.asarray(st["g_c"])
    ang = jnp.asarray(st["ang"])
    cos, sin = jnp.cos(ang), jnp.sin(ang)

    colscale = np.ones((1, N_PROJ), np.float32)
    colscale[0, COL_QA:COL_QA + NSA_HEADS * DK] = DK ** -0.5 * LOG2E
    colscale[0, COL_KR:COL_KR + RET_HEADS * RET_D] = RET_D ** -0.5
    colscale = jnp.asarray(colscale)

    xcur = x.reshape(T, D)
    for l in range(depth):
        w_re = _weight_prep(jnp.transpose(w_in[l]))
        proj = _in_proj(xcur, norm_mix_pre[l][None, :], w_re, colscale)

        xc, vt = _relayout(proj, B, S)
        half = CMP_STRIDE * DK
        pe = jnp.stack([cmp_pe_k[l].reshape(2, half), cmp_pe_v[l].reshape(2, half)])
        pe = jnp.concatenate([pe, jnp.zeros((2, 6, half), F32)], axis=1)
        w1 = jnp.stack([cmp_w1_k[l], cmp_w1_v[l]]).astype(BF16)
        b1 = jnp.stack([cmp_b1_k[l], cmp_b1_v[l]])[:, None, :]
        w2 = jnp.stack([cmp_w2_k[l], cmp_w2_v[l]]).astype(BF16)
        kvc = _compress(xc, pe, w1, b1, w2)

        o_a, o_r = _token_mix(proj, vt, kvc, msel, e_mat, tab, cos, sin, decay, xz, gc, B, S)

        x1, h2 = _out_proj(o_a, o_r, w_out[l].astype(BF16), xcur,
                           norm_mix_post[l][None, :], norm_mlp_pre[l][None, :])
        xcur = _mlp(h2, w_up[l].astype(BF16), w_down[l].astype(BF16), x1, norm_mlp_post[l][None, :])
    return xcur.reshape(B, S, D)
```

```python
import functools

import numpy as np
import jax
import jax.numpy as jnp
from jax import lax
from jax.experimental import pallas as pl
from jax.experimental.pallas import tpu as pltpu

F32 = jnp.float32
BF16 = jnp.bfloat16

NSA_HEADS = 8
NSA_G = 2
NSA_HG = 4
DK = 128
CMP_BLOCK = 32
CMP_STRIDE = 16
SEL_BLOCK = 64
SEL_TOPN = 16
WINDOW = 512
QB = 128
RET_HEADS = 4
RET_D = 256
RET_CHUNK = 128
ROPE_BASE = 10000.0
GN_EPS = 1e-6
NORM_EPS = 1e-6
T5_BUCKETS = 32
T5_MAX_DIST = 128
FORCE = 1e4
NEG = -1e30
M_FLOOR = -1e20
LOG2E = 1.4426950408889634

COL_QA = 0
COL_QR = 1024
COL_KR = 2048
COL_VR = 3072
COL_GR = 4096
COL_KV = 5120
COL_GATE = 6656
N_PROJ = 6912
W_KV = NSA_HEADS * DK
W_GATE = W_KV + 6 * NSA_G * DK
W_RET = W_GATE + 3 * NSA_HEADS
CMP_PAD = 120

VMEM_LIMIT = 56 * 1024 * 1024


def _dot(a, b):
    return jnp.dot(a, b, preferred_element_type=F32)


def _dot_nt(a, b):
    return lax.dot_general(a, b, (((1,), (1,)), ((), ())), preferred_element_type=F32)


def _inproj_kernel(x_ref, g_ref, w_ref, cs_ref, o_ref, h_scr):
    @pl.when(pl.program_id(1) == 0)
    def _():
        xf = x_ref[...]
        ms = jnp.mean(xf * xf, axis=-1, keepdims=True)
        h_scr[...] = (xf * lax.rsqrt(ms + NORM_EPS) * g_ref[...]).astype(BF16)

    acc = _dot_nt(h_scr[...], w_ref[...])
    o_ref[...] = (acc * cs_ref[...]).astype(o_ref.dtype)


PREP_ROWS = 256
PREP_T_RET = COL_QR // PREP_ROWS
PREP_T_KV = COL_KV // PREP_ROWS
PREP_T_GATE = COL_GATE // PREP_ROWS
PREP_SHIFT = W_RET % PREP_ROWS


def _weight_prep_kernel(a_ref, b_ref, o_ref):
    i = pl.program_id(0)
    aligned = (i < PREP_T_RET) | ((i >= PREP_T_KV) & (i < PREP_T_GATE))

    @pl.when(aligned)
    def _():
        o_ref[...] = a_ref[...].astype(BF16)

    @pl.when((i >= PREP_T_RET) & (i < PREP_T_KV))
    def _():
        o_ref[...] = jnp.concatenate([a_ref[PREP_SHIFT:, :], b_ref[:PREP_SHIFT, :]], axis=0).astype(BF16)

    @pl.when(i == PREP_T_GATE)
    def _():
        ng = 3 * NSA_HG
        a = a_ref[:128, :]
        row = lax.broadcasted_iota(jnp.int32, a.shape, 0)
        g0 = jnp.where(row < ng, a, 0.0)
        g1 = jnp.where(row < ng, pltpu.roll(a, 128 - ng, 0), 0.0)
        o_ref[...] = jnp.concatenate([g0, g1], axis=0).astype(BF16)


def _weight_prep(w_t):
    n_in, D = w_t.shape
    assert W_KV % PREP_ROWS == 0 and W_GATE % PREP_ROWS == 0 and N_PROJ - COL_GATE == PREP_ROWS
    src_ret = W_RET // PREP_ROWS - PREP_T_RET
    src_kv = W_KV // PREP_ROWS - PREP_T_KV

    def a_map(i):
        t = jnp.where(i < PREP_T_RET, i,
                      jnp.where(i < PREP_T_KV, i + src_ret,
                                jnp.where(i < PREP_T_GATE, i + src_kv, W_GATE // PREP_ROWS)))
        return (t, 0)

    b_rows = 32
    assert PREP_SHIFT <= b_rows and PREP_ROWS % b_rows == 0

    def b_map(i):
        nxt = (i + src_ret + 1) * (PREP_ROWS // b_rows)
        return (jnp.where((i >= PREP_T_RET) & (i < PREP_T_KV), nxt, 0), 0)

    return pl.pallas_call(
        _weight_prep_kernel,
        grid=(N_PROJ // PREP_ROWS,),
        in_specs=[pl.BlockSpec((PREP_ROWS, D), a_map), pl.BlockSpec((b_rows, D), b_map)],
        out_specs=pl.BlockSpec((PREP_ROWS, D), lambda i: (i, 0)),
        out_shape=jax.ShapeDtypeStruct((N_PROJ, D), BF16),
        compiler_params=pltpu.CompilerParams(
            dimension_semantics=("arbitrary",), vmem_limit_bytes=VMEM_LIMIT),
        name="weight_prep",
    )(w_t, w_t)


def _in_proj(x2, gain, w, colscale, tm=1024, tn=768):
    T, D = x2.shape
    N = w.shape[0]
    tm = min(tm, T)
    return pl.pallas_call(
        _inproj_kernel,
        grid=(T // tm, N // tn),
        in_specs=[
            pl.BlockSpec((tm, D), lambda i, j: (i, 0)),
            pl.BlockSpec((1, D), lambda i, j: (0, 0)),
            pl.BlockSpec((tn, D), lambda i, j: (j, 0)),
            pl.BlockSpec((1, tn), lambda i, j: (0, j)),
        ],
        out_specs=pl.BlockSpec((tm, tn), lambda i, j: (i, j)),
        out_shape=jax.ShapeDtypeStruct((T, N), BF16),
        scratch_shapes=[pltpu.VMEM((tm, D), BF16)],
        compiler_params=pltpu.CompilerParams(
            dimension_semantics=("parallel", "arbitrary"), vmem_limit_bytes=VMEM_LIMIT),
        name="in_proj",
    )(x2, gain, w, colscale)


def _relayout_kernel(cmp_ref, vs_ref, vw_ref, xc_ref, vt_ref, t_scr):
    tm = cmp_ref.shape[0]
    for s in range(4):
        t_scr[s] = cmp_ref[:, s * DK:(s + 1) * DK].astype(F32)
        for t in range(CMP_STRIDE):
            xc_ref[0, s, :, t * DK:(t + 1) * DK] = t_scr[
                s, pl.ds(t, tm // CMP_STRIDE, stride=CMP_STRIDE), :].astype(BF16)
    for kind, ref in enumerate((vs_ref, vw_ref)):
        v = ref[...].astype(F32)
        for g in range(NSA_G):
            for r in range(tm // QB):
                vt_ref[kind, 0, g, r] = v[r * QB:(r + 1) * QB, g * DK:(g + 1) * DK].T.astype(BF16)


def _relayout(proj, B, S, tm=1024):
    tm = min(tm, S)
    rpb = S // tm
    NC, NQ = S // CMP_STRIDE, S // QB
    return pl.pallas_call(
        _relayout_kernel,
        grid=(B * rpb,),
        in_specs=[
            pl.BlockSpec((tm, 4 * DK), lambda i: (i, COL_KV // (4 * DK))),
            pl.BlockSpec((tm, 2 * DK), lambda i: (i, (COL_KV + 6 * DK) // (2 * DK))),
            pl.BlockSpec((tm, 2 * DK), lambda i: (i, (COL_KV + 10 * DK) // (2 * DK))),
        ],
        out_specs=[
            pl.BlockSpec((1, 4, tm // CMP_STRIDE, CMP_STRIDE * DK), lambda i: (i // rpb, 0, i % rpb, 0)),
            pl.BlockSpec((2, 1, NSA_G, tm // QB, DK, QB), lambda i: (0, i // rpb, 0, i % rpb, 0, 0)),
        ],
        out_shape=[jax.ShapeDtypeStruct((B, 4, NC, CMP_STRIDE * DK), BF16),
                   jax.ShapeDtypeStruct((2, B, NSA_G, NQ, DK, QB), BF16)],
        scratch_shapes=[pltpu.VMEM((4, tm, DK), F32)],
        compiler_params=pltpu.CompilerParams(
            dimension_semantics=("parallel",), vmem_limit_bytes=VMEM_LIMIT),
        name="kv_relayout",
    )(proj, proj, proj)


def _compress_kernel(x_ref, pe_ref, w1_ref, b1_ref, w2_ref, o_ref):
    nc = x_ref.shape[2]
    half = x_ref.shape[3]
    xf = x_ref[0, 0].astype(F32)
    xa = (xf + pe_ref[0, 0:1, :]).astype(BF16)
    xb = (xf + pe_ref[0, 1:2, :]).astype(BF16)
    a = _dot(xa, w1_ref[0, :half, :])
    b = _dot(xb, w1_ref[0, half:, :])
    b_next = pltpu.roll(b, nc - 1, 0)
    hid = jax.nn.gelu(a + b_next + b1_ref[0])
    out = _dot(hid.astype(BF16), w2_ref[0])
    o_ref[0, 0, :CMP_PAD, :] = jnp.zeros((CMP_PAD, DK), F32)
    o_ref[0, 0, CMP_PAD:CMP_PAD + nc, :] = out
    o_ref[0, 0, CMP_PAD + nc:, :] = jnp.zeros((o_ref.shape[2] - CMP_PAD - nc, DK), F32)


def _compress(xc, pe, w1, b1, w2):
    B, _, NC, half = xc.shape
    rows = CMP_PAD + NC + 8
    return pl.pallas_call(
        _compress_kernel,
        grid=(B, 4),
        in_specs=[
            pl.BlockSpec((1, 1, NC, half), lambda b, s: (b, s, 0, 0)),
            pl.BlockSpec((1, 8, half), lambda b, s: (s // 2, 0, 0)),
            pl.BlockSpec((1, 2 * half, DK), lambda b, s: (s // 2, 0, 0)),
            pl.BlockSpec((1, 1, DK), lambda b, s: (s // 2, 0, 0)),
            pl.BlockSpec((1, DK, DK), lambda b, s: (s // 2, 0, 0)),
        ],
        out_specs=pl.BlockSpec((1, 1, rows, DK), lambda b, s: (b, s, 0, 0)),
        out_shape=jax.ShapeDtypeStruct((B, 4, rows, DK), F32),
        compiler_params=pltpu.CompilerParams(
            dimension_semantics=("parallel", "parallel"), vmem_limit_bytes=VMEM_LIMIT),
        name="compress_kv",
    )(xc, pe, w1, b1, w2)


TAB_NEAR, TAB_WIN, TAB_CMP, TAB_CONST, TAB_ROWS = 0, 256, 896, 1024, 1152
ONES_ROWS = 16
SEL_UNROLL = 4


def _token_mix_kernel(q_ref, ksl_ref, vslt_ref, kw_ref, vwt_ref, gate_ref, kc_ref, vc_ref, msel_ref, e_ref, tab_ref,
                      rq_ref, rk_ref, rv_ref, rg_ref, cos_ref, sin_ref, dec_ref, xz_ref, gc_ref,
                      o_ref, or_ref, acc_ref, m_ref, l_ref, out_ref, p_ref, a_ref, r_scr):
    qb = pl.program_id(1)
    groups = range(NSA_G)

    @pl.when(qb == 0)
    def _():
        r_scr[...] = jnp.zeros(r_scr.shape, F32)

    retention = functools.partial(
        _retention_heads, q_ref=rq_ref, k_ref=rk_ref, v_ref=rv_ref, g_ref=rg_ref, cos_ref=cos_ref, sin_ref=sin_ref,
        dec_ref=dec_ref, xz_ref=xz_ref, gc_ref=gc_ref, o_ref=or_ref, r_scr=r_scr)
    cols = NSA_HG * QB
    sub = lax.broadcasted_iota(jnp.int32, (128, cols), 0)
    qs_t, g_t = [], []
    for g in groups:
        qf = q_ref[:, g * cols:(g + 1) * cols].astype(F32)
        qs_t.append(jnp.concatenate([qf[:, h * DK:(h + 1) * DK].T for h in range(NSA_HG)],
                                    axis=1).astype(BF16))
        g_t.append(jax.nn.sigmoid(gate_ref[:, g * 128:(g + 1) * 128].astype(F32)).T)

    def gate_row(g, c):
        return jnp.concatenate([g_t[g][3 * h + c:3 * h + c + 1, :] for h in range(NSA_HG)], axis=1)

    retention(range(0, RET_HEADS // 2))

    for g in groups:
        m_ref[g] = jnp.full((1, cols), M_FLOOR, F32)
        l_ref[g] = jnp.zeros((1, cols), F32)
        acc_ref[g] = jnp.zeros((256, cols), F32)

    def cmp_chunk(g, start, bias):
        kc = kc_ref[0, g, pl.ds(start, 128), :].astype(BF16)
        vc_t = vc_ref[0, g, pl.ds(start, 128), :].T
        ms_t = msel_ref[pl.ds(start, 128), :].T
        s = _dot(kc, qs_t[g]) + bias
        m_old = m_ref[g]
        m_new = jnp.maximum(m_old, jnp.max(s, axis=0, keepdims=True))
        alpha = jnp.exp2(m_old - m_new)
        p = jnp.exp2(s - m_new)
        l_ref[g] = alpha * l_ref[g] + jnp.sum(p, axis=0, keepdims=True)
        lhs = jnp.concatenate([vc_t, ms_t], axis=0).astype(BF16)
        acc_ref[g] = alpha * acc_ref[g] + _dot(lhs, p.astype(BF16))
        m_ref[g] = m_new

    near_start = pl.multiple_of(8 * qb, 8)
    near_mask = jnp.where(sub + (8 * qb - CMP_PAD) < 0, NEG, 0.0)
    for g in groups:
        cmp_chunk(g, near_start, tab_ref[g, TAB_CMP:TAB_CMP + 128, :] + near_mask)

    def far_body(c, carry):
        st = pl.multiple_of(CMP_PAD + 128 * c, 8)
        far_mask = jnp.where(128 * c + sub < 8 * qb - CMP_PAD, 0.0, NEG)
        for g in groups:
            cmp_chunk(g, st, tab_ref[g, TAB_CONST:TAB_CONST + 128, :] + far_mask)
        return carry

    lax.fori_loop(0, (8 * qb + 7) // 128, far_body, 0)

    imp_t = []
    for g in groups:
        l = l_ref[g]
        inv = jnp.where(l > 0.0, 1.0 / l, 0.0)
        accv = acc_ref[g]
        out_ref[g] = accv[:DK] * (gate_row(g, 0) * inv)
        u = accv[DK:] * inv
        imp_t.append(u[:, 0:QB] + u[:, QB:2 * QB] + u[:, 2 * QB:3 * QB] + u[:, 3 * QB:4 * QB])

    tiles = [jnp.maximum(qb + d, 0) for d in range(-4, 1)]
    win_ones = jnp.ones((ONES_ROWS, 5 * QB), BF16)
    for g in groups:
        kwin = jnp.concatenate([kw_ref[pl.ds(pl.multiple_of(t * QB, QB), QB), g * DK:(g + 1) * DK]
                                for t in tiles], axis=0)
        vwin_t = jnp.concatenate([vwt_ref[0, 0, g, t] for t in tiles], axis=1)
        lhs = jnp.concatenate([vwin_t, win_ones], axis=0)
        bias = jnp.concatenate(
            [tab_ref[g, TAB_WIN + i * QB:TAB_WIN + (i + 1) * QB, :] + jnp.where(qb + d < 0, NEG, 0.0)
             for i, d in enumerate(range(-4, 1))], axis=0)
        s = _dot(kwin, qs_t[g]) + bias
        p = jnp.exp2(s - jnp.max(s, axis=0, keepdims=True))
        r = _dot(lhs, p.astype(BF16))
        out_ref[g] += gate_row(g, 2) * (r[:DK] / r[DK:DK + 1])

    ji = lax.broadcasted_iota(jnp.int32, (128, NSA_G * QB), 0)
    qi = lax.broadcasted_iota(jnp.int32, (128, NSA_G * QB), 1) & (QB - 1)
    cur = 2 * qb + (qi >= SEL_BLOCK).astype(jnp.int32)
    valid = ji <= cur
    forced = (ji == 0) | (ji == cur) | (ji == cur - 1)
    taken = -jnp.inf
    vt = jnp.where(forced, taken, jnp.where(valid, jnp.concatenate(imp_t, axis=1), -1.0))
    jio = ji.astype(F32)
    for _ in range(SEL_TOPN - 3):
        mx = jnp.max(vt, axis=0, keepdims=True)
        idx = jnp.min(jnp.where(vt == mx, jio, 128.0), axis=0, keepdims=True)
        vt = jnp.where(jio == idx, taken, vt)
    sb_all = jnp.where((vt == taken) & valid, 0.0, NEG)
    sb_far_all = jnp.where(ji < 2 * qb - 2, sb_all, NEG)
    q_far, q_near = [], []
    for g in groups:
        sb_t = sb_all[:, g * QB:(g + 1) * QB]
        sb_far = sb_far_all[:, g * QB:(g + 1) * QB]
        aug_far = jnp.concatenate([sb_far] * NSA_HG, axis=1).astype(BF16)
        aug_near = jnp.concatenate([sb_t] * NSA_HG, axis=1).astype(BF16)
        q_far.append(jnp.concatenate([qs_t[g], aug_far], axis=0))
        q_near.append(jnp.concatenate([qs_t[g], aug_near], axis=0))

    ones_t = jnp.ones((ONES_ROWS, 256), BF16)
    n_tiles = e_ref.shape[0] // 256
    arows = DK + ONES_ROWS

    def k_aug(g, t):
        st = pl.multiple_of(t * QB, QB)
        return jnp.concatenate([ksl_ref[pl.ds(st, QB), g * DK:(g + 1) * DK], e_ref[pl.ds(st, QB), :]], axis=1)

    def scores(g, t):
        tt = jnp.clip(t, 0, n_tiles - 1)
        return _dot(jnp.concatenate([k_aug(g, 2 * tt), k_aug(g, 2 * tt + 1)], axis=0), q_far[g])

    def softmax_stage(g, s, slot):
        m_old = m_ref[g]
        m_new = jnp.maximum(m_old, jnp.max(s, axis=0, keepdims=True))
        a_ref[g, slot] = jnp.exp2(m_old - m_new)
        p_ref[g, slot] = jnp.exp2(s - m_new).astype(BF16)
        m_ref[g] = m_new

    def values_stage(g, v_t, slot):
        lhs = jnp.concatenate([v_t, ones_t], axis=0)
        acc_ref[g, :arows] = a_ref[g, slot] * acc_ref[g, :arows] + _dot(lhs, p_ref[g, slot])

    def v_far(g, t):
        tt = jnp.clip(t, 0, n_tiles - 1)
        return jnp.concatenate([vslt_ref[0, 0, g, 2 * tt], vslt_ref[0, 0, g, 2 * tt + 1]], axis=1)

    n_far = qb // 2
    for g in groups:
        m_ref[g] = jnp.full((1, cols), M_FLOOR, F32)
        acc_ref[g] = jnp.zeros((256, cols), F32)
        softmax_stage(g, scores(g, 0), 0)

    def sel_far_body(v, carry):
        a = SEL_UNROLL * v
        for k in range(SEL_UNROLL):
            cur_slot, other = k % 2, 1 - k % 2
            for g in groups:
                softmax_stage(g, scores(g, a + k + 1), other)
                values_stage(g, v_far(g, a + k), cur_slot)
        return carry

    n_iter = (n_far + SEL_UNROLL - 1) // SEL_UNROLL
    lax.fori_loop(0, n_iter, sel_far_body, 0)

    retention(range(RET_HEADS // 2, RET_HEADS))
    tp = jnp.maximum(qb - 1, 0)
    first_mask = jnp.where(qb == 0, NEG, 0.0)
    for g in groups:
        ka = jnp.concatenate([k_aug(g, tp), k_aug(g, qb)], axis=0)
        bias = jnp.concatenate([tab_ref[g, TAB_NEAR:TAB_NEAR + QB, :] + first_mask,
                                tab_ref[g, TAB_NEAR + QB:TAB_NEAR + 2 * QB, :]], axis=0)
        softmax_stage(g, _dot(ka, q_near[g]) + bias, 0)
        values_stage(g, jnp.concatenate([vslt_ref[0, 0, g, tp], vslt_ref[0, 0, g, qb]], axis=1), 0)
        accv = acc_ref[g, :arows]
        res = out_ref[g] + gate_row(g, 1) * (accv[:DK] / accv[DK:DK + 1])
        for h in range(NSA_HG):
            c0 = (g * NSA_HG + h) * DK
            o_ref[:, c0:c0 + DK] = res[:, h * QB:(h + 1) * QB].T.astype(o_ref.dtype)


def _token_mix(proj, vt, kvc, msel, e_mat, tab, cos, sin, decay, xz, gc, B, S):
    T = B * S
    NQ = S // QB
    cols = NSA_HG * QB
    rows_c = kvc.shape[2]
    once = pl.Buffered(1)
    seq = lambda c0: pl.BlockSpec((S, NSA_G * DK), lambda b, i: (b, c0 // (NSA_G * DK)), pipeline_mode=once)
    vt_spec = lambda kind: pl.BlockSpec((1, 1, NSA_G, NQ, DK, QB), lambda b, i: (kind, b, 0, 0, 0, 0),
                                        pipeline_mode=once)
    const = lambda a: pl.BlockSpec(a.shape, lambda b, i: (0,) * a.ndim, pipeline_mode=once)
    ret_w = RET_HEADS * RET_D
    assert RET_CHUNK == QB
    ret_sec = lambda c0: pl.BlockSpec((RET_CHUNK, ret_w), lambda b, i: (b * NQ + i, c0 // ret_w))
    return pl.pallas_call(
        _token_mix_kernel,
        grid=(B, NQ),
        in_specs=[
            pl.BlockSpec((QB, NSA_HEADS * DK), lambda b, i: (b * NQ + i, 0)),
            seq(COL_KV + 4 * DK), vt_spec(0), seq(COL_KV + 8 * DK), vt_spec(1),
            pl.BlockSpec((QB, NSA_G * 128), lambda b, i: (b * NQ + i, COL_GATE // (NSA_G * 128))),
            pl.BlockSpec((1, NSA_G, rows_c, DK), lambda b, i: (b, 0, 0, 0)),
            pl.BlockSpec((1, NSA_G, rows_c, DK), lambda b, i: (b, 1, 0, 0)),
            const(msel), const(e_mat), const(tab),
            ret_sec(COL_QR), ret_sec(COL_KR), ret_sec(COL_VR), ret_sec(COL_GR),
            pl.BlockSpec((RET_CHUNK, RET_D // 2), lambda b, i: (i, 0)),
            pl.BlockSpec((RET_CHUNK, RET_D // 2), lambda b, i: (i, 0)),
            const(decay), const(xz),
            pl.BlockSpec(memory_space=pltpu.SMEM),
        ],
        out_specs=[pl.BlockSpec((QB, NSA_HEADS * DK), lambda b, i: (b * NQ + i, 0)),
                   pl.BlockSpec((RET_CHUNK, ret_w), lambda b, i: (b * NQ + i, 0))],
        out_shape=[jax.ShapeDtypeStruct((T, NSA_HEADS * DK), BF16),
                   jax.ShapeDtypeStruct((T, ret_w), BF16)],
        scratch_shapes=[
            pltpu.VMEM((NSA_G, 256, cols), F32),
            pltpu.VMEM((NSA_G, 1, cols), F32),
            pltpu.VMEM((NSA_G, 1, cols), F32),
            pltpu.VMEM((NSA_G, DK, cols), F32),
            pltpu.VMEM((NSA_G, 2, 256, cols), BF16),
            pltpu.VMEM((NSA_G, 2, 1, cols), F32),
            pltpu.VMEM((RET_HEADS, RET_D, RET_D), F32),
        ],
        compiler_params=pltpu.CompilerParams(
            dimension_semantics=("parallel", "arbitrary"), vmem_limit_bytes=VMEM_LIMIT),
        name="token_mix",
    )(proj, proj, vt, proj, vt, proj, kvc, kvc, msel, e_mat, tab, proj, proj, proj, proj, cos, sin, decay, xz, gc)


def _retention_heads(heads, q_ref, k_ref, v_ref, g_ref, cos_ref, sin_ref, dec_ref, xz_ref, gc_ref, o_ref, r_scr):
    cos = cos_ref[...]
    sin = sin_ref[...]
    hd = RET_D // 2

    def rot(x):
        x1, x2 = x[:, :hd], x[:, hd:]
        return jnp.concatenate([x1 * cos - x2 * sin, x2 * cos + x1 * sin], axis=1)

    for h in heads:
        sl = slice(h * RET_D, (h + 1) * RET_D)
        qh = rot(q_ref[:, sl].astype(F32))
        kh = rot(k_ref[:, sl].astype(F32))
        vh = v_ref[:, sl]
        xi = xz_ref[h, 0]
        zeta = xz_ref[h, 1]
        qb16 = qh.astype(BF16)
        inner = _dot_nt(qb16, kh.astype(BF16)) * dec_ref[h]
        o = _dot(inner.astype(BF16), vh)
        r_old = r_scr[h]
        cross = _dot(qb16, r_old.astype(BF16))
        o = o + cross * jnp.concatenate([xi, xi], axis=1)
        kz = (kh * jnp.concatenate([zeta, zeta], axis=1)).astype(BF16)
        r_scr[h] = r_old * gc_ref[h] + _dot(kz.T, vh)
        mu = jnp.mean(o, axis=-1, keepdims=True)
        var = jnp.mean(jnp.square(o - mu), axis=-1, keepdims=True)
        on = (o - mu) * lax.rsqrt(var + GN_EPS)
        gf = g_ref[:, sl].astype(F32)
        o_ref[:, sl] = (gf * jax.nn.sigmoid(gf) * on).astype(o_ref.dtype)


def _rms(v, g):
    return v * lax.rsqrt(jnp.mean(v * v, axis=-1, keepdims=True) + NORM_EPS) * g


def _outproj_kernel(oa_ref, or_ref, w_ref, x_ref, gpost_ref, gpre_ref, x1_ref, h2_ref):
    wa = oa_ref.shape[1]
    mix = _dot(oa_ref[...], w_ref[:wa, :]) + _dot(or_ref[...], w_ref[wa:, :])
    x1 = x_ref[...] + _rms(mix, gpost_ref[...])
    x1_ref[...] = x1
    h2_ref[...] = _rms(x1, gpre_ref[...]).astype(h2_ref.dtype)


def _out_proj(oa, orr, w_out, x2, g_post, g_pre, tm=512):
    T, D = x2.shape
    tm = min(tm, T)
    row = lambda w: pl.BlockSpec((tm, w), lambda i: (i, 0))
    full = lambda a: pl.BlockSpec(a.shape, lambda i: (0,) * a.ndim)
    return pl.pallas_call(
        _outproj_kernel,
        grid=(T // tm,),
        in_specs=[row(oa.shape[1]), row(orr.shape[1]), full(w_out), row(D), full(g_post), full(g_pre)],
        out_specs=[row(D), row(D)],
        out_shape=[jax.ShapeDtypeStruct((T, D), F32), jax.ShapeDtypeStruct((T, D), BF16)],
        compiler_params=pltpu.CompilerParams(
            dimension_semantics=("parallel",), vmem_limit_bytes=VMEM_LIMIT),
        name="out_proj",
    )(oa, orr, w_out, x2, g_post, g_pre)


def _mlp_kernel(h_ref, wu_ref, wd_ref, x1_ref, g_ref, o_ref):
    f = pl.program_id(1)

    @pl.when(f == 0)
    def _():
        o_ref[...] = jnp.zeros(o_ref.shape, F32)

    u = jnp.maximum(_dot(h_ref[...], wu_ref[...]), 0.0)
    o_ref[...] += _dot((u * u).astype(BF16), wd_ref[...])

    @pl.when(f == pl.num_programs(1) - 1)
    def _():
        o_ref[...] = x1_ref[...] + _rms(o_ref[...], g_ref[...])


def _mlp(h2, w_up, w_down, x1, g_post, tm=1024, tf=512):
    T, D = x1.shape
    F = w_up.shape[1]
    tm = min(tm, T)
    return pl.pallas_call(
        _mlp_kernel,
        grid=(T // tm, F // tf),
        in_specs=[
            pl.BlockSpec((tm, D), lambda i, f: (i, 0)),
            pl.BlockSpec((D, tf), lambda i, f: (0, f)),
            pl.BlockSpec((tf, D), lambda i, f: (f, 0)),
            pl.BlockSpec((tm, D), lambda i, f: (i, 0), pipeline_mode=pl.Buffered(1)),
            pl.BlockSpec((1, D), lambda i, f: (0, 0)),
        ],
        out_specs=pl.BlockSpec((tm, D), lambda i, f: (i, 0)),
        out_shape=jax.ShapeDtypeStruct((T, D), F32),
        compiler_params=pltpu.CompilerParams(
            dimension_semantics=("parallel", "arbitrary"), vmem_limit_bytes=VMEM_LIMIT),
        name="mlp",
    )(h2, w_up, w_down, x1, g_post)


def _t5_bucket_np(rel):
    n = np.maximum(rel, 0)
    max_exact = T5_BUCKETS // 2
    nf = np.maximum(n, 1).astype(np.float64)
    large = max_exact + (np.log(nf / max_exact) / np.log(T5_MAX_DIST / max_exact)
                         * (T5_BUCKETS - max_exact)).astype(np.int64)
    large = np.minimum(large, T5_BUCKETS - 1)
    return np.where(n < max_exact, n, large)


@functools.lru_cache(maxsize=None)
def _static_tables(S):
    masked = T5_BUCKETS
    i = np.arange(QB)[None, :]

    def bucket_rows(rel, valid):
        return np.where(valid, _t5_bucket_np(rel), masked)

    kk = np.arange(256)[:, None]
    rel = i - kk + QB
    b_near = bucket_rows(rel, rel >= 0) + (T5_BUCKETS + 1)
    kk = np.arange(640)[:, None]
    rel = i - kk + WINDOW
    b_win = bucket_rows(rel, (rel >= 0) & (rel < WINDOW))
    m = np.arange(128)[:, None]
    rel = i - CMP_STRIDE * (m - CMP_PAD) - (CMP_BLOCK - 1)
    b_cmp = bucket_rows(rel, rel >= 0)
    b_const = np.full((128, QB), T5_BUCKETS - 1)
    bucket_idx = np.concatenate([b_near, b_win, b_cmp, b_const], axis=0).astype(np.int32)
    n_cmp = (S - CMP_BLOCK) // CMP_STRIDE + 1
    n_sel = S // SEL_BLOCK
    cs = np.arange(n_cmp) * CMP_STRIDE
    ss = np.arange(n_sel) * SEL_BLOCK
    overlap = (cs[:, None] <= ss[None, :] + SEL_BLOCK - 1) & (cs[:, None] + CMP_BLOCK - 1 >= ss[None, :])
    rows = CMP_PAD + S // CMP_STRIDE + 8
    msel = np.zeros((rows, 128), np.float32)
    msel[CMP_PAD:CMP_PAD + n_cmp, :n_sel] = overlap
    e_mat = (np.arange(S)[:, None] // SEL_BLOCK == np.arange(128)[None, :]).astype(np.float32)
    log_gamma = np.log(1.0 - np.exp2(-5.0 - np.arange(RET_HEADS, dtype=np.float32))).astype(np.float32)
    idx = np.arange(RET_CHUNK, dtype=np.float32)
    diff = idx[:, None] - idx[None, :]
    decay = np.where(diff[None] >= 0, np.exp(np.maximum(diff, 0.0)[None] * log_gamma[:, None, None]), 0.0)
    xi = np.exp((idx + 1.0)[None, :] * log_gamma[:, None])
    zeta = np.exp((RET_CHUNK - 1.0 - idx)[None, :] * log_gamma[:, None])
    xz = np.stack([np.broadcast_to(xi[:, :, None], (RET_HEADS, RET_CHUNK, 128)),
                   np.broadcast_to(zeta[:, :, None], (RET_HEADS, RET_CHUNK, 128))], axis=1)
    g_c = np.exp(RET_CHUNK * log_gamma)
    inv_freq = ROPE_BASE ** (-np.arange(0, RET_D, 2, dtype=np.float32) / RET_D)
    ang = np.arange(S, dtype=np.float32)[:, None] * inv_freq[None, :]
    return dict(bucket_idx=bucket_idx, msel=msel, e_mat=e_mat, decay=decay.astype(np.float32),
                xz=xz.astype(np.float32), g_c=g_c.astype(np.float32), ang=ang.astype(np.float32))


def _bias_table(t5_bias, st):
    assert st["bucket_idx"].shape == (TAB_ROWS, QB)
    idx = jnp.asarray(st["bucket_idx"].reshape(-1))
    onehot = (idx[None, :] == jnp.arange(2 * (T5_BUCKETS + 1), dtype=jnp.int32)[:, None]).astype(F32)
    vals = jnp.concatenate([t5_bias.astype(F32) * LOG2E, jnp.full((1, NSA_HEADS), NEG, F32)], axis=0)
    vals = jnp.concatenate([vals, vals - vals[T5_BUCKETS - 1:T5_BUCKETS]], axis=0)
    tab = jnp.einsum("bh,bn->hn", vals, onehot, precision=lax.Precision.HIGHEST)
    tab = tab.reshape(NSA_G, NSA_HG, TAB_ROWS, QB).transpose(0, 2, 1, 3)
    return tab.reshape(NSA_G, TAB_ROWS, NSA_HG * QB)


def kernel(x, norm_mix_pre, w_in, cmp_pe_k, cmp_w1_k, cmp_b1_k, cmp_w2_k, cmp_pe_v, cmp_w1_v, cmp_b1_v,
           cmp_w2_v, t5_bias, w_out, norm_mix_post, norm_mlp_pre, w_up, w_down, norm_mlp_post):
    B, S, D = x.shape
    T = B * S
    depth = w_in.shape[0]
    st = _static_tables(S)
    tab = _bias_table(t5_bias, st)
    msel = jnp.asarray(st["msel"])
    e_mat = jnp.asarray(st["e_mat"], BF16)
    decay = jnp.asarray(st["decay"])
    xz = jnp.asarray(st["xz"])
    gc = jnp.asarray(st["g_c"])
    ang = jnp.asarray(st["ang"])
    cos, sin = jnp.cos(ang), jnp.sin(ang)

    colscale = np.ones((1, N_PROJ), np.float32)
    colscale[0, COL_QA:COL_QA + NSA_HEADS * DK] = DK ** -0.5 * LOG2E
    colscale[0, COL_KR:COL_KR + RET_HEADS * RET_D] = RET_D ** -0.5
    colscale = jnp.asarray(colscale)

    xcur = x.reshape(T, D)
    for l in range(depth):
        w_re = _weight_prep(jnp.transpose(w_in[l]))
        proj = _in_proj(xcur, norm_mix_pre[l][None, :], w_re, colscale)

        xc, vt = _relayout(proj, B, S)
        half = CMP_STRIDE * DK
        pe = jnp.stack([cmp_pe_k[l].reshape(2, half), cmp_pe_v[l].reshape(2, half)])
        pe = jnp.concatenate([pe, jnp.zeros((2, 6, half), F32)], axis=1)
        w1 = jnp.stack([cmp_w1_k[l], cmp_w1_v[l]]).astype(BF16)
        b1 = jnp.stack([cmp_b1_k[l], cmp_b1_v[l]])[:, None, :]
        w2 = jnp.stack([cmp_w2_k[l], cmp_w2_v[l]]).astype(BF16)
        kvc = _compress(xc, pe, w1, b1, w2)

        o_a, o_r = _token_mix(proj, vt, kvc, msel, e_mat, tab, cos, sin, decay, xz, gc, B, S)

        x1, h2 = _out_proj(o_a, o_r, w_out[l].astype(BF16), xcur,
                           norm_mix_post[l][None, :], norm_mlp_pre[l][None, :])
        xcur = _mlp(h2, w_up[l].astype(BF16), w_down[l].astype(BF16), x1, norm_mlp_post[l][None, :])
    return xcur.reshape(B, S, D)
```

```python
import functools

import numpy as np
import jax
import jax.numpy as jnp
from jax import lax
from jax.experimental import pallas as pl
from jax.experimental.pallas import tpu as pltpu

F32 = jnp.float32
BF16 = jnp.bfloat16

NSA_HEADS = 8
NSA_G = 2
NSA_HG = 4
DK = 128
CMP_BLOCK = 32
CMP_STRIDE = 16
SEL_BLOCK = 64
SEL_TOPN = 16
WINDOW = 512
QB = 128
RET_HEADS = 4
RET_D = 256
RET_CHUNK = 128
ROPE_BASE = 10000.0
GN_EPS = 1e-6
NORM_EPS = 1e-6
T5_BUCKETS = 32
T5_MAX_DIST = 128
FORCE = 1e4
NEG = -1e30
M_FLOOR = -1e20
LOG2E = 1.4426950408889634

COL_QA = 0
COL_QR = 1024
COL_KR = 2048
COL_VR = 3072
COL_GR = 4096
COL_KV = 5120
COL_GATE = 6656
N_PROJ = 6912
W_KV = NSA_HEADS * DK
W_GATE = W_KV + 6 * NSA_G * DK
W_RET = W_GATE + 3 * NSA_HEADS
CMP_PAD = 120

VMEM_LIMIT = 56 * 1024 * 1024


def _dot(a, b):
    return jnp.dot(a, b, preferred_element_type=F32)


def _dot_nt(a, b):
    return lax.dot_general(a, b, (((1,), (1,)), ((), ())), preferred_element_type=F32)


def _inproj_kernel(x_ref, g_ref, w_ref, cs_ref, o_ref, h_scr):
    @pl.when(pl.program_id(1) == 0)
    def _():
        xf = x_ref[...]
        ms = jnp.mean(xf * xf, axis=-1, keepdims=True)
        h_scr[...] = (xf * lax.rsqrt(ms + NORM_EPS) * g_ref[...]).astype(BF16)

    acc = _dot_nt(h_scr[...], w_ref[...])
    o_ref[...] = (acc * cs_ref[...]).astype(o_ref.dtype)


PREP_ROWS = 256
PREP_T_RET = COL_QR // PREP_ROWS
PREP_T_KV = COL_KV // PREP_ROWS
PREP_T_GATE = COL_GATE // PREP_ROWS
PREP_SHIFT = W_RET % PREP_ROWS


def _weight_prep_kernel(a_ref, b_ref, o_ref):
    i = pl.program_id(0)
    aligned = (i < PREP_T_RET) | ((i >= PREP_T_KV) & (i < PREP_T_GATE))

    @pl.when(aligned)
    def _():
        o_ref[...] = a_ref[...].astype(BF16)

    @pl.when((i >= PREP_T_RET) & (i < PREP_T_KV))
    def _():
        o_ref[...] = jnp.concatenate([a_ref[PREP_SHIFT:, :], b_ref[:PREP_SHIFT, :]], axis=0).astype(BF16)

    @pl.when(i == PREP_T_GATE)
    def _():
        ng = 3 * NSA_HG
        a = a_ref[:128, :]
        row = lax.broadcasted_iota(jnp.int32, a.shape, 0)
        g0 = jnp.where(row < ng, a, 0.0)
        g1 = jnp.where(row < ng, pltpu.roll(a, 128 - ng, 0), 0.0)
        o_ref[...] = jnp.concatenate([g0, g1], axis=0).astype(BF16)


def _weight_prep(w_t):
    n_in, D = w_t.shape
    assert W_KV % PREP_ROWS == 0 and W_GATE % PREP_ROWS == 0 and N_PROJ - COL_GATE == PREP_ROWS
    src_ret = W_RET // PREP_ROWS - PREP_T_RET
    src_kv = W_KV // PREP_ROWS - PREP_T_KV

    def a_map(i):
        t = jnp.where(i < PREP_T_RET, i,
                      jnp.where(i < PREP_T_KV, i + src_ret,
                                jnp.where(i < PREP_T_GATE, i + src_kv, W_GATE // PREP_ROWS)))
        return (t, 0)

    b_rows = 32
    assert PREP_SHIFT <= b_rows and PREP_ROWS % b_rows == 0

    def b_map(i):
        nxt = (i + src_ret + 1) * (PREP_ROWS // b_rows)
        return (jnp.where((i >= PREP_T_RET) & (i < PREP_T_KV), nxt, 0), 0)

    return pl.pallas_call(
        _weight_prep_kernel,
        grid=(N_PROJ // PREP_ROWS,),
        in_specs=[pl.BlockSpec((PREP_ROWS, D), a_map), pl.BlockSpec((b_rows, D), b_map)],
        out_specs=pl.BlockSpec((PREP_ROWS, D), lambda i: (i, 0)),
        out_shape=jax.ShapeDtypeStruct((N_PROJ, D), BF16),
        compiler_params=pltpu.CompilerParams(
            dimension_semantics=("arbitrary",), vmem_limit_bytes=VMEM_LIMIT),
        name="weight_prep",
    )(w_t, w_t)


def _in_proj(x2, gain, w, colscale, tm=1024, tn=768):
    T, D = x2.shape
    N = w.shape[0]
    tm = min(tm, T)
    return pl.pallas_call(
        _inproj_kernel,
        grid=(T // tm, N // tn),
        in_specs=[
            pl.BlockSpec((tm, D), lambda i, j: (i, 0)),
            pl.BlockSpec((1, D), lambda i, j: (0, 0)),
            pl.BlockSpec((tn, D), lambda i, j: (j, 0)),
            pl.BlockSpec((1, tn), lambda i, j: (0, j)),
        ],
        out_specs=pl.BlockSpec((tm, tn), lambda i, j: (i, j)),
        out_shape=jax.ShapeDtypeStruct((T, N), BF16),
        scratch_shapes=[pltpu.VMEM((tm, D), BF16)],
        compiler_params=pltpu.CompilerParams(
            dimension_semantics=("parallel", "arbitrary"), vmem_limit_bytes=VMEM_LIMIT),
        name="in_proj",
    )(x2, gain, w, colscale)


def _relayout_kernel(cmp_ref, vs_ref, vw_ref, xc_ref, vt_ref, t_scr):
    tm = cmp_ref.shape[0]
    for s in range(4):
        t_scr[s] = cmp_ref[:, s * DK:(s + 1) * DK].astype(F32)
        for t in range(CMP_STRIDE):
            xc_ref[0, s, :, t * DK:(t + 1) * DK] = t_scr[
                s, pl.ds(t, tm // CMP_STRIDE, stride=CMP_STRIDE), :].astype(BF16)
    for kind, ref in enumerate((vs_ref, vw_ref)):
        v = ref[...].astype(F32)
        for g in range(NSA_G):
            for r in range(tm // QB):
                vt_ref[kind, 0, g, r] = v[r * QB:(r + 1) * QB, g * DK:(g + 1) * DK].T.astype(BF16)


def _relayout(proj, B, S, tm=1024):
    tm = min(tm, S)
    rpb = S // tm
    NC, NQ = S // CMP_STRIDE, S // QB
    return pl.pallas_call(
        _relayout_kernel,
        grid=(B * rpb,),
        in_specs=[
            pl.BlockSpec((tm, 4 * DK), lambda i: (i, COL_KV // (4 * DK))),
            pl.BlockSpec((tm, 2 * DK), lambda i: (i, (COL_KV + 6 * DK) // (2 * DK))),
            pl.BlockSpec((tm, 2 * DK), lambda i: (i, (COL_KV + 10 * DK) // (2 * DK))),
        ],
        out_specs=[
            pl.BlockSpec((1, 4, tm // CMP_STRIDE, CMP_STRIDE * DK), lambda i: (i // rpb, 0, i % rpb, 0)),
            pl.BlockSpec((2, 1, NSA_G, tm // QB, DK, QB), lambda i: (0, i // rpb, 0, i % rpb, 0, 0)),
        ],
        out_shape=[jax.ShapeDtypeStruct((B, 4, NC, CMP_STRIDE * DK), BF16),
                   jax.ShapeDtypeStruct((2, B, NSA_G, NQ, DK, QB), BF16)],
        scratch_shapes=[pltpu.VMEM((4, tm, DK), F32)],
        compiler_params=pltpu.CompilerParams(
            dimension_semantics=("parallel",), vmem_limit_bytes=VMEM_LIMIT),
        name="kv_relayout",
    )(proj, proj, proj)


def _compress_kernel(x_ref, pe_ref, w1_ref, b1_ref, w2_ref, o_ref):
    nc = x_ref.shape[2]
    half = x_ref.shape[3]
    xf = x_ref[0, 0].astype(F32)
    xa = (xf + pe_ref[0, 0:1, :]).astype(BF16)
    xb = (xf + pe_ref[0, 1:2, :]).astype(BF16)
    a = _dot(xa, w1_ref[0, :half, :])
    b = _dot(xb, w1_ref[0, half:, :])
    b_next = pltpu.roll(b, nc - 1, 0)
    hid = jax.nn.gelu(a + b_next + b1_ref[0])
    out = _dot(hid.astype(BF16), w2_ref[0])
    o_ref[0, 0, :CMP_PAD, :] = jnp.zeros((CMP_PAD, DK), F32)
    o_ref[0, 0, CMP_PAD:CMP_PAD + nc, :] = out
    o_ref[0, 0, CMP_PAD + nc:, :] = jnp.zeros((o_ref.shape[2] - CMP_PAD - nc, DK), F32)


def _compress(xc, pe, w1, b1, w2):
    B, _, NC, half = xc.shape
    rows = CMP_PAD + NC + 8
    return pl.pallas_call(
        _compress_kernel,
        grid=(B, 4),
        in_specs=[
            pl.BlockSpec((1, 1, NC, half), lambda b, s: (b, s, 0, 0)),
            pl.BlockSpec((1, 8, half), lambda b, s: (s // 2, 0, 0)),
            pl.BlockSpec((1, 2 * half, DK), lambda b, s: (s // 2, 0, 0)),
            pl.BlockSpec((1, 1, DK), lambda b, s: (s // 2, 0, 0)),
            pl.BlockSpec((1, DK, DK), lambda b, s: (s // 2, 0, 0)),
        ],
        out_specs=pl.BlockSpec((1, 1, rows, DK), lambda b, s: (b, s, 0, 0)),
        out_shape=jax.ShapeDtypeStruct((B, 4, rows, DK), F32),
        compiler_params=pltpu.CompilerParams(
            dimension_semantics=("parallel", "parallel"), vmem_limit_bytes=VMEM_LIMIT),
        name="compress_kv",
    )(xc, pe, w1, b1, w2)


TAB_NEAR, TAB_WIN, TAB_CMP, TAB_CONST, TAB_ROWS = 0, 256, 896, 1024, 1152
ONES_ROWS = 16
SEL_UNROLL = 4


def _token_mix_kernel(q_ref, ksl_ref, vslt_ref, kw_ref, vwt_ref, gate_ref, kc_ref, vc_ref, msel_ref, e_ref, tab_ref,
                      rq_ref, rk_ref, rv_ref, rg_ref, cos_ref, sin_ref, dec_ref, xz_ref, gc_ref,
                      o_ref, or_ref, acc_ref, m_ref, l_ref, out_ref, p_ref, a_ref, r_scr):
    qb = pl.program_id(1)
    groups = range(NSA_G)

    @pl.when(qb == 0)
    def _():
        r_scr[...] = jnp.zeros(r_scr.shape, F32)

    retention = functools.partial(
        _retention_heads, q_ref=rq_ref, k_ref=rk_ref, v_ref=rv_ref, g_ref=rg_ref, cos_ref=cos_ref, sin_ref=sin_ref,
        dec_ref=dec_ref, xz_ref=xz_ref, gc_ref=gc_ref, o_ref=or_ref, r_scr=r_scr)
    cols = NSA_HG * QB
    sub = lax.broadcasted_iota(jnp.int32, (128, cols), 0)
    qs_t, g_t = [], []
    for g in groups:
        qf = q_ref[:, g * cols:(g + 1) * cols].astype(F32)
        qs_t.append(jnp.concatenate([qf[:, h * DK:(h + 1) * DK].T for h in range(NSA_HG)],
                                    axis=1).astype(BF16))
        g_t.append(jax.nn.sigmoid(gate_ref[:, g * 128:(g + 1) * 128].astype(F32)).T)

    def gate_row(g, c):
        return jnp.concatenate([g_t[g][3 * h + c:3 * h + c + 1, :] for h in range(NSA_HG)], axis=1)

    retention(range(0, RET_HEADS // 2))

    for g in groups:
        m_ref[g] = jnp.full((1, cols), M_FLOOR, F32)
        l_ref[g] = jnp.zeros((1, cols), F32)
        acc_ref[g] = jnp.zeros((256, cols), F32)

    def cmp_chunk(g, start, bias):
        kc = kc_ref[0, g, pl.ds(start, 128), :].astype(BF16)
        vc_t = vc_ref[0, g, pl.ds(start, 128), :].T
        ms_t = msel_ref[pl.ds(start, 128), :].T
        s = _dot(kc, qs_t[g]) + bias
        m_old = m_ref[g]
        m_new = jnp.maximum(m_old, jnp.max(s, axis=0, keepdims=True))
        alpha = jnp.exp2(m_old - m_new)
        p = jnp.exp2(s - m_new)
        l_ref[g] = alpha * l_ref[g] + jnp.sum(p, axis=0, keepdims=True)
        lhs = jnp.concatenate([vc_t, ms_t], axis=0).astype(BF16)
        acc_ref[g] = alpha * acc_ref[g] + _dot(lhs, p.astype(BF16))
        m_ref[g] = m_new

    near_start = pl.multiple_of(8 * qb, 8)
    near_mask = jnp.where(sub + (8 * qb - CMP_PAD) < 0, NEG, 0.0)
    for g in groups:
        cmp_chunk(g, near_start, tab_ref[g, TAB_CMP:TAB_CMP + 128, :] + near_mask)

    def far_body(c, carry):
        st = pl.multiple_of(CMP_PAD + 128 * c, 8)
        far_mask = jnp.where(128 * c + sub < 8 * qb - CMP_PAD, 0.0, NEG)
        for g in groups:
            cmp_chunk(g, st, tab_ref[g, TAB_CONST:TAB_CONST + 128, :] + far_mask)
        return carry

    lax.fori_loop(0, (8 * qb + 7) // 128, far_body, 0)

    imp_t = []
    for g in groups:
        l = l_ref[g]
        inv = jnp.where(l > 0.0, 1.0 / l, 0.0)
        accv = acc_ref[g]
        out_ref[g] = accv[:DK] * (gate_row(g, 0) * inv)
        u = accv[DK:] * inv
        imp_t.append(u[:, 0:QB] + u[:, QB:2 * QB] + u[:, 2 * QB:3 * QB] + u[:, 3 * QB:4 * QB])

    tiles = [jnp.maximum(qb + d, 0) for d in range(-4, 1)]
    win_ones = jnp.ones((ONES_ROWS, 5 * QB), BF16)
    for g in groups:
        kwin = jnp.concatenate([kw_ref[pl.ds(pl.multiple_of(t * QB, QB), QB), g * DK:(g + 1) * DK]
                                for t in tiles], axis=0)
        vwin_t = jnp.concatenate([vwt_ref[0, 0, g, t] for t in tiles], axis=1)
        lhs = jnp.concatenate([vwin_t, win_ones], axis=0)
        bias = jnp.concatenate(
            [tab_ref[g, TAB_WIN + i * QB:TAB_WIN + (i + 1) * QB, :] + jnp.where(qb + d < 0, NEG, 0.0)
             for i, d in enumerate(range(-4, 1))], axis=0)
        s = _dot(kwin, qs_t[g]) + bias
        p = jnp.exp2(s - jnp.max(s, axis=0, keepdims=True))
        r = _dot(lhs, p.astype(BF16))
        out_ref[g] += gate_row(g, 2) * (r[:DK] / r[DK:DK + 1])

    ji = lax.broadcasted_iota(jnp.int32, (128, NSA_G * QB), 0)
    qi = lax.broadcasted_iota(jnp.int32, (128, NSA_G * QB), 1) & (QB - 1)
    cur = 2 * qb + (qi >= SEL_BLOCK).astype(jnp.int32)
    valid = ji <= cur
    forced = (ji == 0) | (ji == cur) | (ji == cur - 1)
    taken = -jnp.inf
    vt = jnp.where(forced, taken, jnp.where(valid, jnp.concatenate(imp_t, axis=1), -1.0))
    jio = ji.astype(F32)
    for _ in range(SEL_TOPN - 3):
        mx = jnp.max(vt, axis=0, keepdims=True)
        idx = jnp.min(jnp.where(vt == mx, jio, 128.0), axis=0, keepdims=True)
        vt = jnp.where(jio == idx, taken, vt)
    sb_all = jnp.where((vt == taken) & valid, 0.0, NEG)
    sb_far_all = jnp.where(ji < 2 * qb - 2, sb_all, NEG)
    q_far, q_near = [], []
    for g in groups:
        sb_t = sb_all[:, g * QB:(g + 1) * QB]
        sb_far = sb_far_all[:, g * QB:(g + 1) * QB]
        aug_far = jnp.concatenate([sb_far] * NSA_HG, axis=1).astype(BF16)
        aug_near = jnp.concatenate([sb_t] * NSA_HG, axis=1).astype(BF16)
        q_far.append(jnp.concatenate([qs_t[g], aug_far], axis=0))
        q_near.append(jnp.concatenate([qs_t[g], aug_near], axis=0))

    ones_t = jnp.ones((ONES_ROWS, 256), BF16)
    n_tiles = e_ref.shape[0] // 256
    arows = DK + ONES_ROWS

    def k_aug(g, t):
        st = pl.multiple_of(t * QB, QB)
        return jnp.concatenate([ksl_ref[pl.ds(st, QB), g * DK:(g + 1) * DK], e_ref[pl.ds(st, QB), :]], axis=1)

    def scores(g, t):
        tt = jnp.clip(t, 0, n_tiles - 1)
        return _dot(jnp.concatenate([k_aug(g, 2 * tt), k_aug(g, 2 * tt + 1)], axis=0), q_far[g])

    def softmax_stage(g, s, slot):
        m_old = m_ref[g]
        m_new = jnp.maximum(m_old, jnp.max(s, axis=0, keepdims=True))
        a_ref[g, slot] = jnp.exp2(m_old - m_new)
        p_ref[g, slot] = jnp.exp2(s - m_new).astype(BF16)
        m_ref[g] = m_new

    def values_stage(g, v_t, slot):
        lhs = jnp.concatenate([v_t, ones_t], axis=0)
        acc_ref[g, :arows] = a_ref[g, slot] * acc_ref[g, :arows] + _dot(lhs, p_ref[g, slot])

    def v_far(g, t):
        tt = jnp.clip(t, 0, n_tiles - 1)
        return jnp.concatenate([vslt_ref[0, 0, g, 2 * tt], vslt_ref[0, 0, g, 2 * tt + 1]], axis=1)

    n_far = qb // 2
    for g in groups:
        m_ref[g] = jnp.full((1, cols), M_FLOOR, F32)
        acc_ref[g] = jnp.zeros((256, cols), F32)
        softmax_stage(g, scores(g, 0), 0)

    def far_tiles(a, count):
        for k in range(count):
            cur_slot, other = k % 2, 1 - k % 2
            for g in groups:
                softmax_stage(g, scores(g, a + k + 1), other)
                values_stage(g, v_far(g, a + k), cur_slot)

    def sel_far_body(v, carry):
        far_tiles(SEL_UNROLL * v, SEL_UNROLL)
        return carry

    n_main = n_far // SEL_UNROLL
    lax.fori_loop(0, n_main, sel_far_body, 0)
    base = SEL_UNROLL * n_main

    def sel_rest_body(v, carry):
        far_tiles(base + 2 * v, 2)
        return carry

    lax.fori_loop(0, (n_far - base + 1) // 2, sel_rest_body, 0)

    retention(range(RET_HEADS // 2, RET_HEADS))
    tp = jnp.maximum(qb - 1, 0)
    first_mask = jnp.where(qb == 0, NEG, 0.0)
    for g in groups:
        ka = jnp.concatenate([k_aug(g, tp), k_aug(g, qb)], axis=0)
        bias = jnp.concatenate([tab_ref[g, TAB_NEAR:TAB_NEAR + QB, :] + first_mask,
                                tab_ref[g, TAB_NEAR + QB:TAB_NEAR + 2 * QB, :]], axis=0)
        softmax_stage(g, _dot(ka, q_near[g]) + bias, 0)
        values_stage(g, jnp.concatenate([vslt_ref[0, 0, g, tp], vslt_ref[0, 0, g, qb]], axis=1), 0)
        accv = acc_ref[g, :arows]
        res = out_ref[g] + gate_row(g, 1) * (accv[:DK] / accv[DK:DK + 1])
        for h in range(NSA_HG):
            c0 = (g * NSA_HG + h) * DK
            o_ref[:, c0:c0 + DK] = res[:, h * QB:(h + 1) * QB].T.astype(o_ref.dtype)


def _token_mix(proj, vt, kvc, msel, e_mat, tab, cos, sin, decay, xz, gc, B, S):
    T = B * S
    NQ = S // QB
    cols = NSA_HG * QB
    rows_c = kvc.shape[2]
    once = pl.Buffered(1)
    seq = lambda c0: pl.BlockSpec((S, NSA_G * DK), lambda b, i: (b, c0 // (NSA_G * DK)), pipeline_mode=once)
    vt_spec = lambda kind: pl.BlockSpec((1, 1, NSA_G, NQ, DK, QB), lambda b, i: (kind, b, 0, 0, 0, 0),
                                        pipeline_mode=once)
    const = lambda a: pl.BlockSpec(a.shape, lambda b, i: (0,) * a.ndim, pipeline_mode=once)
    ret_w = RET_HEADS * RET_D
    assert RET_CHUNK == QB
    ret_sec = lambda c0: pl.BlockSpec((RET_CHUNK, ret_w), lambda b, i: (b * NQ + i, c0 // ret_w))
    return pl.pallas_call(
        _token_mix_kernel,
        grid=(B, NQ),
        in_specs=[
            pl.BlockSpec((QB, NSA_HEADS * DK), lambda b, i: (b * NQ + i, 0)),
            seq(COL_KV + 4 * DK), vt_spec(0), seq(COL_KV + 8 * DK), vt_spec(1),
            pl.BlockSpec((QB, NSA_G * 128), lambda b, i: (b * NQ + i, COL_GATE // (NSA_G * 128))),
            pl.BlockSpec((1, NSA_G, rows_c, DK), lambda b, i: (b, 0, 0, 0)),
            pl.BlockSpec((1, NSA_G, rows_c, DK), lambda b, i: (b, 1, 0, 0)),
            const(msel), const(e_mat), const(tab),
            ret_sec(COL_QR), ret_sec(COL_KR), ret_sec(COL_VR), ret_sec(COL_GR),
            pl.BlockSpec((RET_CHUNK, RET_D // 2), lambda b, i: (i, 0)),
            pl.BlockSpec((RET_CHUNK, RET_D // 2), lambda b, i: (i, 0)),
            const(decay), const(xz),
            pl.BlockSpec(memory_space=pltpu.SMEM),
        ],
        out_specs=[pl.BlockSpec((QB, NSA_HEADS * DK), lambda b, i: (b * NQ + i, 0)),
                   pl.BlockSpec((RET_CHUNK, ret_w), lambda b, i: (b * NQ + i, 0))],
        out_shape=[jax.ShapeDtypeStruct((T, NSA_HEADS * DK), BF16),
                   jax.ShapeDtypeStruct((T, ret_w), BF16)],
        scratch_shapes=[
            pltpu.VMEM((NSA_G, 256, cols), F32),
            pltpu.VMEM((NSA_G, 1, cols), F32),
            pltpu.VMEM((NSA_G, 1, cols), F32),
            pltpu.VMEM((NSA_G, DK, cols), F32),
            pltpu.VMEM((NSA_G, 2, 256, cols), BF16),
            pltpu.VMEM((NSA_G, 2, 1, cols), F32),
            pltpu.VMEM((RET_HEADS, RET_D, RET_D), F32),
        ],
        compiler_params=pltpu.CompilerParams(
            dimension_semantics=("parallel", "arbitrary"), vmem_limit_bytes=VMEM_LIMIT),
        name="token_mix",
    )(proj, proj, vt, proj, vt, proj, kvc, kvc, msel, e_mat, tab, proj, proj, proj, proj, cos, sin, decay, xz, gc)


def _retention_heads(heads, q_ref, k_ref, v_ref, g_ref, cos_ref, sin_ref, dec_ref, xz_ref, gc_ref, o_ref, r_scr):
    cos = cos_ref[...]
    sin = sin_ref[...]
    hd = RET_D // 2

    def rot(x):
        x1, x2 = x[:, :hd], x[:, hd:]
        return jnp.concatenate([x1 * cos - x2 * sin, x2 * cos + x1 * sin], axis=1)

    for h in heads:
        sl = slice(h * RET_D, (h + 1) * RET_D)
        qh = rot(q_ref[:, sl].astype(F32))
        kh = rot(k_ref[:, sl].astype(F32))
        vh = v_ref[:, sl]
        xi = xz_ref[h, 0]
        zeta = xz_ref[h, 1]
        qb16 = qh.astype(BF16)
        inner = _dot_nt(qb16, kh.astype(BF16)) * dec_ref[h]
        o = _dot(inner.astype(BF16), vh)
        r_old = r_scr[h]
        cross = _dot(qb16, r_old.astype(BF16))
        o = o + cross * jnp.concatenate([xi, xi], axis=1)
        kz = (kh * jnp.concatenate([zeta, zeta], axis=1)).astype(BF16)
        r_scr[h] = r_old * gc_ref[h] + _dot(kz.T, vh)
        mu = jnp.mean(o, axis=-1, keepdims=True)
        var = jnp.mean(jnp.square(o - mu), axis=-1, keepdims=True)
        on = (o - mu) * lax.rsqrt(var + GN_EPS)
        gf = g_ref[:, sl].astype(F32)
        o_ref[:, sl] = (gf * jax.nn.sigmoid(gf) * on).astype(o_ref.dtype)


def _rms(v, g):
    return v * lax.rsqrt(jnp.mean(v * v, axis=-1, keepdims=True) + NORM_EPS) * g


def _outproj_kernel(oa_ref, or_ref, w_ref, x_ref, gpost_ref, gpre_ref, x1_ref, h2_ref):
    wa = oa_ref.shape[1]
    mix = _dot(oa_ref[...], w_ref[:wa, :]) + _dot(or_ref[...], w_ref[wa:, :])
    x1 = x_ref[...] + _rms(mix, gpost_ref[...])
    x1_ref[...] = x1
    h2_ref[...] = _rms(x1, gpre_ref[...]).astype(h2_ref.dtype)


def _out_proj(oa, orr, w_out, x2, g_post, g_pre, tm=512):
    T, D = x2.shape
    tm = min(tm, T)
    row = lambda w: pl.BlockSpec((tm, w), lambda i: (i, 0))
    full = lambda a: pl.BlockSpec(a.shape, lambda i: (0,) * a.ndim)
    return pl.pallas_call(
        _outproj_kernel,
        grid=(T // tm,),
        in_specs=[row(oa.shape[1]), row(orr.shape[1]), full(w_out), row(D), full(g_post), full(g_pre)],
        out_specs=[row(D), row(D)],
        out_shape=[jax.ShapeDtypeStruct((T, D), F32), jax.ShapeDtypeStruct((T, D), BF16)],
        compiler_params=pltpu.CompilerParams(
            dimension_semantics=("parallel",), vmem_limit_bytes=VMEM_LIMIT),
        name="out_proj",
    )(oa, orr, w_out, x2, g_post, g_pre)


def _mlp_kernel(h_ref, wu_ref, wd_ref, x1_ref, g_ref, o_ref):
    f = pl.program_id(1)

    @pl.when(f == 0)
    def _():
        o_ref[...] = jnp.zeros(o_ref.shape, F32)

    u = jnp.maximum(_dot(h_ref[...], wu_ref[...]), 0.0)
    o_ref[...] += _dot((u * u).astype(BF16), wd_ref[...])

    @pl.when(f == pl.num_programs(1) - 1)
    def _():
        o_ref[...] = x1_ref[...] + _rms(o_ref[...], g_ref[...])


def _mlp(h2, w_up, w_down, x1, g_post, tm=1024, tf=512):
    T, D = x1.shape
    F = w_up.shape[1]
    tm = min(tm, T)
    return pl.pallas_call(
        _mlp_kernel,
        grid=(T // tm, F // tf),
        in_specs=[
            pl.BlockSpec((tm, D), lambda i, f: (i, 0)),
            pl.BlockSpec((D, tf), lambda i, f: (0, f)),
            pl.BlockSpec((tf, D), lambda i, f: (f, 0)),
            pl.BlockSpec((tm, D), lambda i, f: (i, 0), pipeline_mode=pl.Buffered(1)),
            pl.BlockSpec((1, D), lambda i, f: (0, 0)),
        ],
        out_specs=pl.BlockSpec((tm, D), lambda i, f: (i, 0)),
        out_shape=jax.ShapeDtypeStruct((T, D), F32),
        compiler_params=pltpu.CompilerParams(
            dimension_semantics=("parallel", "arbitrary"), vmem_limit_bytes=VMEM_LIMIT),
        name="mlp",
    )(h2, w_up, w_down, x1, g_post)


def _t5_bucket_np(rel):
    n = np.maximum(rel, 0)
    max_exact = T5_BUCKETS // 2
    nf = np.maximum(n, 1).astype(np.float64)
    large = max_exact + (np.log(nf / max_exact) / np.log(T5_MAX_DIST / max_exact)
                         * (T5_BUCKETS - max_exact)).astype(np.int64)
    large = np.minimum(large, T5_BUCKETS - 1)
    return np.where(n < max_exact, n, large)


@functools.lru_cache(maxsize=None)
def _static_tables(S):
    masked = T5_BUCKETS
    i = np.arange(QB)[None, :]

    def bucket_rows(rel, valid):
        return np.where(valid, _t5_bucket_np(rel), masked)

    kk = np.arange(256)[:, None]
    rel = i - kk + QB
    b_near = bucket_rows(rel, rel >= 0) + (T5_BUCKETS + 1)
    kk = np.arange(640)[:, None]
    rel = i - kk + WINDOW
    b_win = bucket_rows(rel, (rel >= 0) & (rel < WINDOW))
    m = np.arange(128)[:, None]
    rel = i - CMP_STRIDE * (m - CMP_PAD) - (CMP_BLOCK - 1)
    b_cmp = bucket_rows(rel, rel >= 0)
    b_const = np.full((128, QB), T5_BUCKETS - 1)
    bucket_idx = np.concatenate([b_near, b_win, b_cmp, b_const], axis=0).astype(np.int32)
    n_cmp = (S - CMP_BLOCK) // CMP_STRIDE + 1
    n_sel = S // SEL_BLOCK
    cs = np.arange(n_cmp) * CMP_STRIDE
    ss = np.arange(n_sel) * SEL_BLOCK
    overlap = (cs[:, None] <= ss[None, :] + SEL_BLOCK - 1) & (cs[:, None] + CMP_BLOCK - 1 >= ss[None, :])
    rows = CMP_PAD + S // CMP_STRIDE + 8
    msel = np.zeros((rows, 128), np.float32)
    msel[CMP_PAD:CMP_PAD + n_cmp, :n_sel] = overlap
    e_mat = (np.arange(S)[:, None] // SEL_BLOCK == np.arange(128)[None, :]).astype(np.float32)
    log_gamma = np.log(1.0 - np.exp2(-5.0 - np.arange(RET_HEADS, dtype=np.float32))).astype(np.float32)
    idx = np.arange(RET_CHUNK, dtype=np.float32)
    diff = idx[:, None] - idx[None, :]
    decay = np.where(diff[None] >= 0, np.exp(np.maximum(diff, 0.0)[None] * log_gamma[:, None, None]), 0.0)
    xi = np.exp((idx + 1.0)[None, :] * log_gamma[:, None])
    zeta = np.exp((RET_CHUNK - 1.0 - idx)[None, :] * log_gamma[:, None])
    xz = np.stack([np.broadcast_to(xi[:, :, None], (RET_HEADS, RET_CHUNK, 128)),
                   np.broadcast_to(zeta[:, :, None], (RET_HEADS, RET_CHUNK, 128))], axis=1)
    g_c = np.exp(RET_CHUNK * log_gamma)
    inv_freq = ROPE_BASE ** (-np.arange(0, RET_D, 2, dtype=np.float32) / RET_D)
    ang = np.arange(S, dtype=np.float32)[:, None] * inv_freq[None, :]
    return dict(bucket_idx=bucket_idx, msel=msel, e_mat=e_mat, decay=decay.astype(np.float32),
                xz=xz.astype(np.float32), g_c=g_c.astype(np.float32), ang=ang.astype(np.float32))


def _bias_table(t5_bias, st):
    assert st["bucket_idx"].shape == (TAB_ROWS, QB)
    idx = jnp.asarray(st["bucket_idx"].reshape(-1))
    onehot = (idx[None, :] == jnp.arange(2 * (T5_BUCKETS + 1), dtype=jnp.int32)[:, None]).astype(F32)
    vals = jnp.concatenate([t5_bias.astype(F32) * LOG2E, jnp.full((1, NSA_HEADS), NEG, F32)], axis=0)
    vals = jnp.concatenate([vals, vals - vals[T5_BUCKETS - 1:T5_BUCKETS]], axis=0)
    tab = jnp.einsum("bh,bn->hn", vals, onehot, precision=lax.Precision.HIGHEST)
    tab = tab.reshape(NSA_G, NSA_HG, TAB_ROWS, QB).transpose(0, 2, 1, 3)
    return tab.reshape(NSA_G, TAB_ROWS, NSA_HG * QB)


def kernel(x, norm_mix_pre, w_in, cmp_pe_k, cmp_w1_k, cmp_b1_k, cmp_w2_k, cmp_pe_v, cmp_w1_v, cmp_b1_v,
           cmp_w2_v, t5_bias, w_out, norm_mix_post, norm_mlp_pre, w_up, w_down, norm_mlp_post):
    B, S, D = x.shape
    T = B * S
    depth = w_in.shape[0]
    st = _static_tables(S)
    tab = _bias_table(t5_bias, st)
    msel = jnp.asarray(st["msel"])
    e_mat = jnp.asarray(st["e_mat"], BF16)
    decay = jnp.asarray(st["decay"])
    xz = jnp.asarray(st["xz"])
    gc = jnp.asarray(st["g_c"])
    ang = jnp.asarray(st["ang"])
    cos, sin = jnp.cos(ang), jnp.sin(ang)

    colscale = np.ones((1, N_PROJ), np.float32)
    colscale[0, COL_QA:COL_QA + NSA_HEADS * DK] = DK ** -0.5 * LOG2E
    colscale[0, COL_KR:COL_KR + RET_HEADS * RET_D] = RET_D ** -0.5
    colscale = jnp.asarray(colscale)

    xcur = x.reshape(T, D)
    for l in range(depth):
        w_re = _weight_prep(jnp.transpose(w_in[l]))
        proj = _in_proj(xcur, norm_mix_pre[l][None, :], w_re, colscale)

        xc, vt = _relayout(proj, B, S)
        half = CMP_STRIDE * DK
        pe = jnp.stack([cmp_pe_k[l].reshape(2, half), cmp_pe_v[l].reshape(2, half)])
        pe = jnp.concatenate([pe, jnp.zeros((2, 6, half), F32)], axis=1)
        w1 = jnp.stack([cmp_w1_k[l], cmp_w1_v[l]]).astype(BF16)
        b1 = jnp.stack([cmp_b1_k[l], cmp_b1_v[l]])[:, None, :]
        w2 = jnp.stack([cmp_w2_k[l], cmp_w2_v[l]]).astype(BF16)
        kvc = _compress(xc, pe, w1, b1, w2)

        o_a, o_r = _token_mix(proj, vt, kvc, msel, e_mat, tab, cos, sin, decay, xz, gc, B, S)

        x1, h2 = _out_proj(o_a, o_r, w_out[l].astype(BF16), xcur,
                           norm_mix_post[l][None, :], norm_mlp_pre[l][None, :])
        xcur = _mlp(h2, w_up[l].astype(BF16), w_down[l].astype(BF16), x1, norm_mlp_post[l][None, :])
    return xcur.reshape(B, S, D)
```

```python
import functools

import numpy as np
import jax
import jax.numpy as jnp
from jax import lax
from jax.experimental import pallas as pl
from jax.experimental.pallas import tpu as pltpu

F32 = jnp.float32
BF16 = jnp.bfloat16

NSA_HEADS = 8
NSA_G = 2
NSA_HG = 4
DK = 128
CMP_BLOCK = 32
CMP_STRIDE = 16
SEL_BLOCK = 64
SEL_TOPN = 16
WINDOW = 512
QB = 128
RET_HEADS = 4
RET_D = 256
RET_CHUNK = 128
ROPE_BASE = 10000.0
GN_EPS = 1e-6
NORM_EPS = 1e-6
T5_BUCKETS = 32
T5_MAX_DIST = 128
FORCE = 1e4
NEG = -1e30
M_FLOOR = -1e20
LOG2E = 1.4426950408889634

COL_QA = 0
COL_QR = 1024
COL_KR = 2048
COL_VR = 3072
COL_GR = 4096
COL_KV = 5120
COL_GATE = 6656
N_PROJ = 6912
W_KV = NSA_HEADS * DK
W_GATE = W_KV + 6 * NSA_G * DK
W_RET = W_GATE + 3 * NSA_HEADS
CMP_PAD = 120

VMEM_LIMIT = 56 * 1024 * 1024


def _dot(a, b):
    return jnp.dot(a, b, preferred_element_type=F32)


def _dot_nt(a, b):
    return lax.dot_general(a, b, (((1,), (1,)), ((), ())), preferred_element_type=F32)


def _inproj_kernel(x_ref, g_ref, w_ref, cs_ref, o_ref, h_scr):
    @pl.when(pl.program_id(1) == 0)
    def _():
        xf = x_ref[...]
        ms = jnp.mean(xf * xf, axis=-1, keepdims=True)
        h_scr[...] = (xf * lax.rsqrt(ms + NORM_EPS) * g_ref[...]).astype(BF16)

    acc = _dot_nt(h_scr[...], w_ref[...])
    o_ref[...] = (acc * cs_ref[...]).astype(o_ref.dtype)


PREP_ROWS = 256
PREP_T_RET = COL_QR // PREP_ROWS
PREP_T_KV = COL_KV // PREP_ROWS
PREP_T_GATE = COL_GATE // PREP_ROWS
PREP_SHIFT = W_RET % PREP_ROWS


def _weight_prep_kernel(a_ref, b_ref, o_ref):
    i = pl.program_id(0)
    aligned = (i < PREP_T_RET) | ((i >= PREP_T_KV) & (i < PREP_T_GATE))

    @pl.when(aligned)
    def _():
        o_ref[...] = a_ref[...].astype(BF16)

    @pl.when((i >= PREP_T_RET) & (i < PREP_T_KV))
    def _():
        o_ref[...] = jnp.concatenate([a_ref[PREP_SHIFT:, :], b_ref[:PREP_SHIFT, :]], axis=0).astype(BF16)

    @pl.when(i == PREP_T_GATE)
    def _():
        ng = 3 * NSA_HG
        a = a_ref[:128, :]
        row = lax.broadcasted_iota(jnp.int32, a.shape, 0)
        g0 = jnp.where(row < ng, a, 0.0)
        g1 = jnp.where(row < ng, pltpu.roll(a, 128 - ng, 0), 0.0)
        o_ref[...] = jnp.concatenate([g0, g1], axis=0).astype(BF16)


def _weight_prep(w_t):
    n_in, D = w_t.shape
    assert W_KV % PREP_ROWS == 0 and W_GATE % PREP_ROWS == 0 and N_PROJ - COL_GATE == PREP_ROWS
    src_ret = W_RET // PREP_ROWS - PREP_T_RET
    src_kv = W_KV // PREP_ROWS - PREP_T_KV

    def a_map(i):
        t = jnp.where(i < PREP_T_RET, i,
                      jnp.where(i < PREP_T_KV, i + src_ret,
                                jnp.where(i < PREP_T_GATE, i + src_kv, W_GATE // PREP_ROWS)))
        return (t, 0)

    b_rows = 32
    assert PREP_SHIFT <= b_rows and PREP_ROWS % b_rows == 0

    def b_map(i):
        nxt = (i + src_ret + 1) * (PREP_ROWS // b_rows)
        return (jnp.where((i >= PREP_T_RET) & (i < PREP_T_KV), nxt, 0), 0)

    return pl.pallas_call(
        _weight_prep_kernel,
        grid=(N_PROJ // PREP_ROWS,),
        in_specs=[pl.BlockSpec((PREP_ROWS, D), a_map), pl.BlockSpec((b_rows, D), b_map)],
        out_specs=pl.BlockSpec((PREP_ROWS, D), lambda i: (i, 0)),
        out_shape=jax.ShapeDtypeStruct((N_PROJ, D), BF16),
        compiler_params=pltpu.CompilerParams(
            dimension_semantics=("arbitrary",), vmem_limit_bytes=VMEM_LIMIT),
        name="weight_prep",
    )(w_t, w_t)


def _in_proj(x2, gain, w, colscale, tm=1024, tn=768):
    T, D = x2.shape
    N = w.shape[0]
    tm = min(tm, T)
    return pl.pallas_call(
        _inproj_kernel,
        grid=(T // tm, N // tn),
        in_specs=[
            pl.BlockSpec((tm, D), lambda i, j: (i, 0)),
            pl.BlockSpec((1, D), lambda i, j: (0, 0)),
            pl.BlockSpec((tn, D), lambda i, j: (j, 0)),
            pl.BlockSpec((1, tn), lambda i, j: (0, j)),
        ],
        out_specs=pl.BlockSpec((tm, tn), lambda i, j: (i, j)),
        out_shape=jax.ShapeDtypeStruct((T, N), BF16),
        scratch_shapes=[pltpu.VMEM((tm, D), BF16)],
        compiler_params=pltpu.CompilerParams(
            dimension_semantics=("parallel", "arbitrary"), vmem_limit_bytes=VMEM_LIMIT),
        name="in_proj",
    )(x2, gain, w, colscale)


def _relayout_kernel(cmp_ref, vs_ref, vw_ref, xc_ref, vt_ref, t_scr):
    tm = cmp_ref.shape[0]
    for s in range(4):
        t_scr[s] = cmp_ref[:, s * DK:(s + 1) * DK].astype(F32)
        for t in range(CMP_STRIDE):
            xc_ref[0, s, :, t * DK:(t + 1) * DK] = t_scr[
                s, pl.ds(t, tm // CMP_STRIDE, stride=CMP_STRIDE), :].astype(BF16)
    for kind, ref in enumerate((vs_ref, vw_ref)):
        v = ref[...].astype(F32)
        for g in range(NSA_G):
            for r in range(tm // QB):
                vt_ref[kind, 0, g, r] = v[r * QB:(r + 1) * QB, g * DK:(g + 1) * DK].T.astype(BF16)


def _relayout(proj, B, S, tm=1024):
    tm = min(tm, S)
    rpb = S // tm
    NC, NQ = S // CMP_STRIDE, S // QB
    return pl.pallas_call(
        _relayout_kernel,
        grid=(B * rpb,),
        in_specs=[
            pl.BlockSpec((tm, 4 * DK), lambda i: (i, COL_KV // (4 * DK))),
            pl.BlockSpec((tm, 2 * DK), lambda i: (i, (COL_KV + 6 * DK) // (2 * DK))),
            pl.BlockSpec((tm, 2 * DK), lambda i: (i, (COL_KV + 10 * DK) // (2 * DK))),
        ],
        out_specs=[
            pl.BlockSpec((1, 4, tm // CMP_STRIDE, CMP_STRIDE * DK), lambda i: (i // rpb, 0, i % rpb, 0)),
            pl.BlockSpec((2, 1, NSA_G, tm // QB, DK, QB), lambda i: (0, i // rpb, 0, i % rpb, 0, 0)),
        ],
        out_shape=[jax.ShapeDtypeStruct((B, 4, NC, CMP_STRIDE * DK), BF16),
                   jax.ShapeDtypeStruct((2, B, NSA_G, NQ, DK, QB), BF16)],
        scratch_shapes=[pltpu.VMEM((4, tm, DK), F32)],
        compiler_params=pltpu.CompilerParams(
            dimension_semantics=("parallel",), vmem_limit_bytes=VMEM_LIMIT),
        name="kv_relayout",
    )(proj, proj, proj)


def _compress_kernel(x_ref, pe_ref, w1_ref, b1_ref, w2_ref, o_ref):
    nc = x_ref.shape[2]
    half = x_ref.shape[3]
    xf = x_ref[0, 0].astype(F32)
    xa = (xf + pe_ref[0, 0:1, :]).astype(BF16)
    xb = (xf + pe_ref[0, 1:2, :]).astype(BF16)
    a = _dot(xa, w1_ref[0, :half, :])
    b = _dot(xb, w1_ref[0, half:, :])
    b_next = pltpu.roll(b, nc - 1, 0)
    hid = jax.nn.gelu(a + b_next + b1_ref[0])
    out = _dot(hid.astype(BF16), w2_ref[0])
    o_ref[0, 0, :CMP_PAD, :] = jnp.zeros((CMP_PAD, DK), F32)
    o_ref[0, 0, CMP_PAD:CMP_PAD + nc, :] = out
    o_ref[0, 0, CMP_PAD + nc:, :] = jnp.zeros((o_ref.shape[2] - CMP_PAD - nc, DK), F32)


def _compress(xc, pe, w1, b1, w2):
    B, _, NC, half = xc.shape
    rows = CMP_PAD + NC + 8
    return pl.pallas_call(
        _compress_kernel,
        grid=(B, 4),
        in_specs=[
            pl.BlockSpec((1, 1, NC, half), lambda b, s: (b, s, 0, 0)),
            pl.BlockSpec((1, 8, half), lambda b, s: (s // 2, 0, 0)),
            pl.BlockSpec((1, 2 * half, DK), lambda b, s: (s // 2, 0, 0)),
            pl.BlockSpec((1, 1, DK), lambda b, s: (s // 2, 0, 0)),
            pl.BlockSpec((1, DK, DK), lambda b, s: (s // 2, 0, 0)),
        ],
        out_specs=pl.BlockSpec((1, 1, rows, DK), lambda b, s: (b, s, 0, 0)),
        out_shape=jax.ShapeDtypeStruct((B, 4, rows, DK), F32),
        compiler_params=pltpu.CompilerParams(
            dimension_semantics=("parallel", "parallel"), vmem_limit_bytes=VMEM_LIMIT),
        name="compress_kv",
    )(xc, pe, w1, b1, w2)


TAB_NEAR, TAB_WIN, TAB_CMP, TAB_CONST, TAB_ROWS = 0, 256, 896, 1024, 1152
ONES_ROWS = 16
SEL_UNROLLS = (8, 4, 2)


def _token_mix_kernel(q_ref, ksl_ref, vslt_ref, kw_ref, vwt_ref, gate_ref, kc_ref, vc_ref, msel_ref, e_ref, tab_ref,
                      rq_ref, rk_ref, rv_ref, rg_ref, cos_ref, sin_ref, dec_ref, xz_ref, gc_ref,
                      o_ref, or_ref, acc_ref, m_ref, l_ref, out_ref, p_ref, a_ref, r_scr):
    qb = pl.program_id(1)
    groups = range(NSA_G)

    @pl.when(qb == 0)
    def _():
        r_scr[...] = jnp.zeros(r_scr.shape, F32)

    retention = functools.partial(
        _retention_heads, q_ref=rq_ref, k_ref=rk_ref, v_ref=rv_ref, g_ref=rg_ref, cos_ref=cos_ref, sin_ref=sin_ref,
        dec_ref=dec_ref, xz_ref=xz_ref, gc_ref=gc_ref, o_ref=or_ref, r_scr=r_scr)
    cols = NSA_HG * QB
    sub = lax.broadcasted_iota(jnp.int32, (128, cols), 0)
    qs_t, g_t = [], []
    for g in groups:
        qf = q_ref[:, g * cols:(g + 1) * cols].astype(F32)
        qs_t.append(jnp.concatenate([qf[:, h * DK:(h + 1) * DK].T for h in range(NSA_HG)],
                                    axis=1).astype(BF16))
        g_t.append(jax.nn.sigmoid(gate_ref[:, g * 128:(g + 1) * 128].astype(F32)).T)

    def gate_row(g, c):
        return jnp.concatenate([g_t[g][3 * h + c:3 * h + c + 1, :] for h in range(NSA_HG)], axis=1)

    retention(range(0, RET_HEADS // 2))

    for g in groups:
        m_ref[g] = jnp.full((1, cols), M_FLOOR, F32)
        l_ref[g] = jnp.zeros((1, cols), F32)
        acc_ref[g] = jnp.zeros((256, cols), F32)

    def cmp_chunk(g, start, bias):
        kc = kc_ref[0, g, pl.ds(start, 128), :].astype(BF16)
        vc_t = vc_ref[0, g, pl.ds(start, 128), :].T
        ms_t = msel_ref[pl.ds(start, 128), :].T
        s = _dot(kc, qs_t[g]) + bias
        m_old = m_ref[g]
        m_new = jnp.maximum(m_old, jnp.max(s, axis=0, keepdims=True))
        alpha = jnp.exp2(m_old - m_new)
        p = jnp.exp2(s - m_new)
        l_ref[g] = alpha * l_ref[g] + jnp.sum(p, axis=0, keepdims=True)
        lhs = jnp.concatenate([vc_t, ms_t], axis=0).astype(BF16)
        acc_ref[g] = alpha * acc_ref[g] + _dot(lhs, p.astype(BF16))
        m_ref[g] = m_new

    near_start = pl.multiple_of(8 * qb, 8)
    near_mask = jnp.where(sub + (8 * qb - CMP_PAD) < 0, NEG, 0.0)
    for g in groups:
        cmp_chunk(g, near_start, tab_ref[g, TAB_CMP:TAB_CMP + 128, :] + near_mask)

    def far_body(c, carry):
        st = pl.multiple_of(CMP_PAD + 128 * c, 8)
        far_mask = jnp.where(128 * c + sub < 8 * qb - CMP_PAD, 0.0, NEG)
        for g in groups:
            cmp_chunk(g, st, tab_ref[g, TAB_CONST:TAB_CONST + 128, :] + far_mask)
        return carry

    lax.fori_loop(0, (8 * qb + 7) // 128, far_body, 0)

    imp_t = []
    for g in groups:
        l = l_ref[g]
        inv = jnp.where(l > 0.0, 1.0 / l, 0.0)
        accv = acc_ref[g]
        out_ref[g] = accv[:DK] * (gate_row(g, 0) * inv)
        u = accv[DK:] * inv
        imp_t.append(u[:, 0:QB] + u[:, QB:2 * QB] + u[:, 2 * QB:3 * QB] + u[:, 3 * QB:4 * QB])

    tiles = [jnp.maximum(qb + d, 0) for d in range(-4, 1)]
    win_ones = jnp.ones((ONES_ROWS, 5 * QB), BF16)
    for g in groups:
        kwin = jnp.concatenate([kw_ref[pl.ds(pl.multiple_of(t * QB, QB), QB), g * DK:(g + 1) * DK]
                                for t in tiles], axis=0)
        vwin_t = jnp.concatenate([vwt_ref[0, 0, g, t] for t in tiles], axis=1)
        lhs = jnp.concatenate([vwin_t, win_ones], axis=0)
        bias = jnp.concatenate(
            [tab_ref[g, TAB_WIN + i * QB:TAB_WIN + (i + 1) * QB, :] + jnp.where(qb + d < 0, NEG, 0.0)
             for i, d in enumerate(range(-4, 1))], axis=0)
        s = _dot(kwin, qs_t[g]) + bias
        p = jnp.exp2(s - jnp.max(s, axis=0, keepdims=True))
        r = _dot(lhs, p.astype(BF16))
        out_ref[g] += gate_row(g, 2) * (r[:DK] / r[DK:DK + 1])

    ji = lax.broadcasted_iota(jnp.int32, (128, NSA_G * QB), 0)
    qi = lax.broadcasted_iota(jnp.int32, (128, NSA_G * QB), 1) & (QB - 1)
    cur = 2 * qb + (qi >= SEL_BLOCK).astype(jnp.int32)
    valid = ji <= cur
    forced = (ji == 0) | (ji == cur) | (ji == cur - 1)
    taken = -jnp.inf
    vt = jnp.where(forced, taken, jnp.where(valid, jnp.concatenate(imp_t, axis=1), -1.0))
    jio = ji.astype(F32)
    for _ in range(SEL_TOPN - 3):
        mx = jnp.max(vt, axis=0, keepdims=True)
        idx = jnp.min(jnp.where(vt == mx, jio, 128.0), axis=0, keepdims=True)
        vt = jnp.where(jio == idx, taken, vt)
    sb_all = jnp.where((vt == taken) & valid, 0.0, NEG)
    sb_far_all = jnp.where(ji < 2 * qb - 2, sb_all, NEG)
    q_far, q_near = [], []
    for g in groups:
        sb_t = sb_all[:, g * QB:(g + 1) * QB]
        sb_far = sb_far_all[:, g * QB:(g + 1) * QB]
        aug_far = jnp.concatenate([sb_far] * NSA_HG, axis=1).astype(BF16)
        aug_near = jnp.concatenate([sb_t] * NSA_HG, axis=1).astype(BF16)
        q_far.append(jnp.concatenate([qs_t[g], aug_far], axis=0))
        q_near.append(jnp.concatenate([qs_t[g], aug_near], axis=0))

    ones_t = jnp.ones((ONES_ROWS, 256), BF16)
    n_tiles = e_ref.shape[0] // 256
    arows = DK + ONES_ROWS

    def k_aug(g, t):
        st = pl.multiple_of(t * QB, QB)
        return jnp.concatenate([ksl_ref[pl.ds(st, QB), g * DK:(g + 1) * DK], e_ref[pl.ds(st, QB), :]], axis=1)

    def scores(g, t):
        tt = jnp.clip(t, 0, n_tiles - 1)
        return _dot(jnp.concatenate([k_aug(g, 2 * tt), k_aug(g, 2 * tt + 1)], axis=0), q_far[g])

    def softmax_stage(g, s, slot):
        m_old = m_ref[g]
        m_new = jnp.maximum(m_old, jnp.max(s, axis=0, keepdims=True))
        a_ref[g, slot] = jnp.exp2(m_old - m_new)
        p_ref[g, slot] = jnp.exp2(s - m_new).astype(BF16)
        m_ref[g] = m_new

    def values_stage(g, v_t, slot):
        lhs = jnp.concatenate([v_t, ones_t], axis=0)
        acc_ref[g, :arows] = a_ref[g, slot] * acc_ref[g, :arows] + _dot(lhs, p_ref[g, slot])

    def v_far(g, t):
        tt = jnp.clip(t, 0, n_tiles - 1)
        return jnp.concatenate([vslt_ref[0, 0, g, 2 * tt], vslt_ref[0, 0, g, 2 * tt + 1]], axis=1)

    n_far = qb // 2
    for g in groups:
        m_ref[g] = jnp.full((1, cols), M_FLOOR, F32)
        acc_ref[g] = jnp.zeros((256, cols), F32)
        softmax_stage(g, scores(g, 0), 0)

    def far_tiles(a, count):
        for k in range(count):
            cur_slot, other = k % 2, 1 - k % 2
            for g in groups:
                softmax_stage(g, scores(g, a + k + 1), other)
                values_stage(g, v_far(g, a + k), cur_slot)

    done = 0
    for count in SEL_UNROLLS:
        trips = (n_far - done) // count if count > 2 else (n_far - done + 1) // 2

        def body(v, carry, done=done, count=count):
            far_tiles(done + count * v, count)
            return carry

        lax.fori_loop(0, trips, body, 0)
        done = done + count * trips

    retention(range(RET_HEADS // 2, RET_HEADS))
    tp = jnp.maximum(qb - 1, 0)
    first_mask = jnp.where(qb == 0, NEG, 0.0)
    for g in groups:
        ka = jnp.concatenate([k_aug(g, tp), k_aug(g, qb)], axis=0)
        bias = jnp.concatenate([tab_ref[g, TAB_NEAR:TAB_NEAR + QB, :] + first_mask,
                                tab_ref[g, TAB_NEAR + QB:TAB_NEAR + 2 * QB, :]], axis=0)
        softmax_stage(g, _dot(ka, q_near[g]) + bias, 0)
        values_stage(g, jnp.concatenate([vslt_ref[0, 0, g, tp], vslt_ref[0, 0, g, qb]], axis=1), 0)
        accv = acc_ref[g, :arows]
        res = out_ref[g] + gate_row(g, 1) * (accv[:DK] / accv[DK:DK + 1])
        for h in range(NSA_HG):
            c0 = (g * NSA_HG + h) * DK
            o_ref[:, c0:c0 + DK] = res[:, h * QB:(h + 1) * QB].T.astype(o_ref.dtype)


def _token_mix(proj, vt, kvc, msel, e_mat, tab, cos, sin, decay, xz, gc, B, S):
    T = B * S
    NQ = S // QB
    cols = NSA_HG * QB
    rows_c = kvc.shape[2]
    once = pl.Buffered(1)
    seq = lambda c0: pl.BlockSpec((S, NSA_G * DK), lambda b, i: (b, c0 // (NSA_G * DK)), pipeline_mode=once)
    vt_spec = lambda kind: pl.BlockSpec((1, 1, NSA_G, NQ, DK, QB), lambda b, i: (kind, b, 0, 0, 0, 0),
                                        pipeline_mode=once)
    const = lambda a: pl.BlockSpec(a.shape, lambda b, i: (0,) * a.ndim, pipeline_mode=once)
    ret_w = RET_HEADS * RET_D
    assert RET_CHUNK == QB
    ret_sec = lambda c0: pl.BlockSpec((RET_CHUNK, ret_w), lambda b, i: (b * NQ + i, c0 // ret_w))
    return pl.pallas_call(
        _token_mix_kernel,
        grid=(B, NQ),
        in_specs=[
            pl.BlockSpec((QB, NSA_HEADS * DK), lambda b, i: (b * NQ + i, 0)),
            seq(COL_KV + 4 * DK), vt_spec(0), seq(COL_KV + 8 * DK), vt_spec(1),
            pl.BlockSpec((QB, NSA_G * 128), lambda b, i: (b * NQ + i, COL_GATE // (NSA_G * 128))),
            pl.BlockSpec((1, NSA_G, rows_c, DK), lambda b, i: (b, 0, 0, 0)),
            pl.BlockSpec((1, NSA_G, rows_c, DK), lambda b, i: (b, 1, 0, 0)),
            const(msel), const(e_mat), const(tab),
            ret_sec(COL_QR), ret_sec(COL_KR), ret_sec(COL_VR), ret_sec(COL_GR),
            pl.BlockSpec((RET_CHUNK, RET_D // 2), lambda b, i: (i, 0)),
            pl.BlockSpec((RET_CHUNK, RET_D // 2), lambda b, i: (i, 0)),
            const(decay), const(xz),
            pl.BlockSpec(memory_space=pltpu.SMEM),
        ],
        out_specs=[pl.BlockSpec((QB, NSA_HEADS * DK), lambda b, i: (b * NQ + i, 0)),
                   pl.BlockSpec((RET_CHUNK, ret_w), lambda b, i: (b * NQ + i, 0))],
        out_shape=[jax.ShapeDtypeStruct((T, NSA_HEADS * DK), BF16),
                   jax.ShapeDtypeStruct((T, ret_w), BF16)],
        scratch_shapes=[
            pltpu.VMEM((NSA_G, 256, cols), F32),
            pltpu.VMEM((NSA_G, 1, cols), F32),
            pltpu.VMEM((NSA_G, 1, cols), F32),
            pltpu.VMEM((NSA_G, DK, cols), F32),
            pltpu.VMEM((NSA_G, 2, 256, cols), BF16),
            pltpu.VMEM((NSA_G, 2, 1, cols), F32),
            pltpu.VMEM((RET_HEADS, RET_D, RET_D), F32),
        ],
        compiler_params=pltpu.CompilerParams(
            dimension_semantics=("parallel", "arbitrary"), vmem_limit_bytes=VMEM_LIMIT),
        name="token_mix",
    )(proj, proj, vt, proj, vt, proj, kvc, kvc, msel, e_mat, tab, proj, proj, proj, proj, cos, sin, decay, xz, gc)


def _retention_heads(heads, q_ref, k_ref, v_ref, g_ref, cos_ref, sin_ref, dec_ref, xz_ref, gc_ref, o_ref, r_scr):
    cos = cos_ref[...]
    sin = sin_ref[...]
    hd = RET_D // 2

    def rot(x):
        x1, x2 = x[:, :hd], x[:, hd:]
        return jnp.concatenate([x1 * cos - x2 * sin, x2 * cos + x1 * sin], axis=1)

    for h in heads:
        sl = slice(h * RET_D, (h + 1) * RET_D)
        qh = rot(q_ref[:, sl].astype(F32))
        kh = rot(k_ref[:, sl].astype(F32))
        vh = v_ref[:, sl]
        xi = xz_ref[h, 0]
        zeta = xz_ref[h, 1]
        qb16 = qh.astype(BF16)
        inner = _dot_nt(qb16, kh.astype(BF16)) * dec_ref[h]
        o = _dot(inner.astype(BF16), vh)
        r_old = r_scr[h]
        cross = _dot(qb16, r_old.astype(BF16))
        o = o + cross * jnp.concatenate([xi, xi], axis=1)
        kz = (kh * jnp.concatenate([zeta, zeta], axis=1)).astype(BF16)
        r_scr[h] = r_old * gc_ref[h] + _dot(kz.T, vh)
        mu = jnp.mean(o, axis=-1, keepdims=True)
        var = jnp.mean(jnp.square(o - mu), axis=-1, keepdims=True)
        on = (o - mu) * lax.rsqrt(var + GN_EPS)
        gf = g_ref[:, sl].astype(F32)
        o_ref[:, sl] = (gf * jax.nn.sigmoid(gf) * on).astype(o_ref.dtype)


def _rms(v, g):
    return v * lax.rsqrt(jnp.mean(v * v, axis=-1, keepdims=True) + NORM_EPS) * g


def _outproj_kernel(oa_ref, or_ref, w_ref, x_ref, gpost_ref, gpre_ref, x1_ref, h2_ref):
    wa = oa_ref.shape[1]
    mix = _dot(oa_ref[...], w_ref[:wa, :]) + _dot(or_ref[...], w_ref[wa:, :])
    x1 = x_ref[...] + _rms(mix, gpost_ref[...])
    x1_ref[...] = x1
    h2_ref[...] = _rms(x1, gpre_ref[...]).astype(h2_ref.dtype)


def _out_proj(oa, orr, w_out, x2, g_post, g_pre, tm=512):
    T, D = x2.shape
    tm = min(tm, T)
    row = lambda w: pl.BlockSpec((tm, w), lambda i: (i, 0))
    full = lambda a: pl.BlockSpec(a.shape, lambda i: (0,) * a.ndim)
    return pl.pallas_call(
        _outproj_kernel,
        grid=(T // tm,),
        in_specs=[row(oa.shape[1]), row(orr.shape[1]), full(w_out), row(D), full(g_post), full(g_pre)],
        out_specs=[row(D), row(D)],
        out_shape=[jax.ShapeDtypeStruct((T, D), F32), jax.ShapeDtypeStruct((T, D), BF16)],
        compiler_params=pltpu.CompilerParams(
            dimension_semantics=("parallel",), vmem_limit_bytes=VMEM_LIMIT),
        name="out_proj",
    )(oa, orr, w_out, x2, g_post, g_pre)


def _mlp_kernel(h_ref, wu_ref, wd_ref, x1_ref, g_ref, o_ref):
    f = pl.program_id(1)

    @pl.when(f == 0)
    def _():
        o_ref[...] = jnp.zeros(o_ref.shape, F32)

    u = jnp.maximum(_dot(h_ref[...], wu_ref[...]), 0.0)
    o_ref[...] += _dot((u * u).astype(BF16), wd_ref[...])

    @pl.when(f == pl.num_programs(1) - 1)
    def _():
        o_ref[...] = x1_ref[...] + _rms(o_ref[...], g_ref[...])


def _mlp(h2, w_up, w_down, x1, g_post, tm=1024, tf=512):
    T, D = x1.shape
    F = w_up.shape[1]
    tm = min(tm, T)
    return pl.pallas_call(
        _mlp_kernel,
        grid=(T // tm, F // tf),
        in_specs=[
            pl.BlockSpec((tm, D), lambda i, f: (i, 0)),
            pl.BlockSpec((D, tf), lambda i, f: (0, f)),
            pl.BlockSpec((tf, D), lambda i, f: (f, 0)),
            pl.BlockSpec((tm, D), lambda i, f: (i, 0), pipeline_mode=pl.Buffered(1)),
            pl.BlockSpec((1, D), lambda i, f: (0, 0)),
        ],
        out_specs=pl.BlockSpec((tm, D), lambda i, f: (i, 0)),
        out_shape=jax.ShapeDtypeStruct((T, D), F32),
        compiler_params=pltpu.CompilerParams(
            dimension_semantics=("parallel", "arbitrary"), vmem_limit_bytes=VMEM_LIMIT),
        name="mlp",
    )(h2, w_up, w_down, x1, g_post)


def _t5_bucket_np(rel):
    n = np.maximum(rel, 0)
    max_exact = T5_BUCKETS // 2
    nf = np.maximum(n, 1).astype(np.float64)
    large = max_exact + (np.log(nf / max_exact) / np.log(T5_MAX_DIST / max_exact)
                         * (T5_BUCKETS - max_exact)).astype(np.int64)
    large = np.minimum(large, T5_BUCKETS - 1)
    return np.where(n < max_exact, n, large)


@functools.lru_cache(maxsize=None)
def _static_tables(S):
    masked = T5_BUCKETS
    i = np.arange(QB)[None, :]

    def bucket_rows(rel, valid):
        return np.where(valid, _t5_bucket_np(rel), masked)

    kk = np.arange(256)[:, None]
    rel = i - kk + QB
    b_near = bucket_rows(rel, rel >= 0) + (T5_BUCKETS + 1)
    kk = np.arange(640)[:, None]
    rel = i - kk + WINDOW
    b_win = bucket_rows(rel, (rel >= 0) & (rel < WINDOW))
    m = np.arange(128)[:, None]
    rel = i - CMP_STRIDE * (m - CMP_PAD) - (CMP_BLOCK - 1)
    b_cmp = bucket_rows(rel, rel >= 0)
    b_const = np.full((128, QB), T5_BUCKETS - 1)
    bucket_idx = np.concatenate([b_near, b_win, b_cmp, b_const], axis=0).astype(np.int32)
    n_cmp = (S - CMP_BLOCK) // CMP_STRIDE + 1
    n_sel = S // SEL_BLOCK
    cs = np.arange(n_cmp) * CMP_STRIDE
    ss = np.arange(n_sel) * SEL_BLOCK
    overlap = (cs[:, None] <= ss[None, :] + SEL_BLOCK - 1) & (cs[:, None] + CMP_BLOCK - 1 >= ss[None, :])
    rows = CMP_PAD + S // CMP_STRIDE + 8
    msel = np.zeros((rows, 128), np.float32)
    msel[CMP_PAD:CMP_PAD + n_cmp, :n_sel] = overlap
    e_mat = (np.arange(S)[:, None] // SEL_BLOCK == np.arange(128)[None, :]).astype(np.float32)
    log_gamma = np.log(1.0 - np.exp2(-5.0 - np.arange(RET_HEADS, dtype=np.float32))).astype(np.float32)
    idx = np.arange(RET_CHUNK, dtype=np.float32)
    diff = idx[:, None] - idx[None, :]
    decay = np.where(diff[None] >= 0, np.exp(np.maximum(diff, 0.0)[None] * log_gamma[:, None, None]), 0.0)
    xi = np.exp((idx + 1.0)[None, :] * log_gamma[:, None])
    zeta = np.exp((RET_CHUNK - 1.0 - idx)[None, :] * log_gamma[:, None])
    xz = np.stack([np.broadcast_to(xi[:, :, None], (RET_HEADS, RET_CHUNK, 128)),
                   np.broadcast_to(zeta[:, :, None], (RET_HEADS, RET_CHUNK, 128))], axis=1)
    g_c = np.exp(RET_CHUNK * log_gamma)
    inv_freq = ROPE_BASE ** (-np.arange(0, RET_D, 2, dtype=np.float32) / RET_D)
    ang = np.arange(S, dtype=np.float32)[:, None] * inv_freq[None, :]
    return dict(bucket_idx=bucket_idx, msel=msel, e_mat=e_mat, decay=decay.astype(np.float32),
                xz=xz.astype(np.float32), g_c=g_c.astype(np.float32), ang=ang.astype(np.float32))


def _bias_table(t5_bias, st):
    assert st["bucket_idx"].shape == (TAB_ROWS, QB)
    idx = jnp.asarray(st["bucket_idx"].reshape(-1))
    onehot = (idx[None, :] == jnp.arange(2 * (T5_BUCKETS + 1), dtype=jnp.int32)[:, None]).astype(F32)
    vals = jnp.concatenate([t5_bias.astype(F32) * LOG2E, jnp.full((1, NSA_HEADS), NEG, F32)], axis=0)
    vals = jnp.concatenate([vals, vals - vals[T5_BUCKETS - 1:T5_BUCKETS]], axis=0)
    tab = jnp.einsum("bh,bn->hn", vals, onehot, precision=lax.Precision.HIGHEST)
    tab = tab.reshape(NSA_G, NSA_HG, TAB_ROWS, QB).transpose(0, 2, 1, 3)
    return tab.reshape(NSA_G, TAB_ROWS, NSA_HG * QB)


def kernel(x, norm_mix_pre, w_in, cmp_pe_k, cmp_w1_k, cmp_b1_k, cmp_w2_k, cmp_pe_v, cmp_w1_v, cmp_b1_v,
           cmp_w2_v, t5_bias, w_out, norm_mix_post, norm_mlp_pre, w_up, w_down, norm_mlp_post):
    B, S, D = x.shape
    T = B * S
    depth = w_in.shape[0]
    st = _static_tables(S)
    tab = _bias_table(t5_bias, st)
    msel = jnp.asarray(st["msel"])
    e_mat = jnp.asarray(st["e_mat"], BF16)
    decay = jnp.asarray(st["decay"])
    xz = jnp.asarray(st["xz"])
    gc = jnp.asarray(st["g_c"])
    ang = jnp.asarray(st["ang"])
    cos, sin = jnp.cos(ang), jnp.sin(ang)

    colscale = np.ones((1, N_PROJ), np.float32)
    colscale[0, COL_QA:COL_QA + NSA_HEADS * DK] = DK ** -0.5 * LOG2E
    colscale[0, COL_KR:COL_KR + RET_HEADS * RET_D] = RET_D ** -0.5
    colscale = jnp.asarray(colscale)

    xcur = x.reshape(T, D)
    for l in range(depth):
        w_re = _weight_prep(jnp.transpose(w_in[l]))
        proj = _in_proj(xcur, norm_mix_pre[l][None, :], w_re, colscale)

        xc, vt = _relayout(proj, B, S)
        half = CMP_STRIDE * DK
        pe = jnp.stack([cmp_pe_k[l].reshape(2, half), cmp_pe_v[l].reshape(2, half)])
        pe = jnp.concatenate([pe, jnp.zeros((2, 6, half), F32)], axis=1)
        w1 = jnp.stack([cmp_w1_k[l], cmp_w1_v[l]]).astype(BF16)
        b1 = jnp.stack([cmp_b1_k[l], cmp_b1_v[l]])[:, None, :]
        w2 = jnp.stack([cmp_w2_k[l], cmp_w2_v[l]]).astype(BF16)
        kvc = _compress(xc, pe, w1, b1, w2)

        o_a, o_r = _token_mix(proj, vt, kvc, msel, e_mat, tab, cos, sin, decay, xz, gc, B, S)

        x1, h2 = _out_proj(o_a, o_r, w_out[l].astype(BF16), xcur,
                           norm_mix_post[l][None, :], norm_mlp_pre[l][None, :])
        xcur = _mlp(h2, w_up[l].astype(BF16), w_down[l].astype(BF16), x1, norm_mlp_post[l][None, :])
    return xcur.reshape(B, S, D)
```
